```python
import math
import jax, jax.numpy as jnp
from jax import lax
import numpy as np

D_MODEL = 1024
BATCH = 8
SEQ = 2048
DEPTH = 4
DEC_BATCH = 128
DEC_SEQ = 8
PAST_LEN = 16384
PAGE_SIZE = 128

MIX_WIDTH = D_MODEL
CONV_CH = MIX_WIDTH // 2
RWKV_DIM = MIX_WIDTH - CONV_CH
RWKV_HEAD = 64
RWKV_HEADS = RWKV_DIM // RWKV_HEAD
CONV_WIDTH = 31
CONV_BUF = CONV_WIDTH - 1
DECAY_RANK = 64
ICLR_RANK = 64
GATE_RANK = 128
RWKV_COLS = 3 * RWKV_DIM + DECAY_RANK + ICLR_RANK + GATE_RANK
IN_COLS = 2 * CONV_CH + RWKV_COLS
RWKV_SPLITS = (RWKV_DIM, 2 * RWKV_DIM, 3 * RWKV_DIM,
               3 * RWKV_DIM + DECAY_RANK, 3 * RWKV_DIM + DECAY_RANK + ICLR_RANK)
N_MEM = 256
MEM_HEADS = 4
MEM_HEAD_DIM = D_MODEL // MEM_HEADS
D_FF = 4 * D_MODEL
RMS_EPS = 1e-6
LN_EPS = 1e-5
GN_EPS = 64e-5

kernel_name = "hymba_conformer_rwkv7_memxattn_step"


def rmsnorm(x, g):
    xf = x.astype(jnp.float32)
    y = xf * lax.rsqrt(jnp.mean(xf * xf, axis=-1, keepdims=True) + RMS_EPS)
    return (y * g.astype(jnp.float32)).astype(x.dtype)


def layernorm(x, g, b):
    xf = x.astype(jnp.float32)
    mu = jnp.mean(xf, axis=-1, keepdims=True)
    var = jnp.mean(jnp.square(xf - mu), axis=-1, keepdims=True)
    y = (xf - mu) * lax.rsqrt(var + LN_EPS)
    return (y * g.astype(jnp.float32) + b.astype(jnp.float32)).astype(x.dtype)


def head_groupnorm(o, g, b):
    mu = jnp.mean(o, axis=-1, keepdims=True)
    var = jnp.mean(jnp.square(o - mu), axis=-1, keepdims=True)
    y = (o - mu) * lax.rsqrt(var + GN_EPS)
    return (y * g.astype(jnp.float32).reshape(RWKV_HEADS, RWKV_HEAD)
            + b.astype(jnp.float32).reshape(RWKV_HEADS, RWKV_HEAD))


def conformer_conv_group(z_val, z_gate, conv_buf, conv_w, conv_b, ln_g, ln_b):
    u = z_val * jax.nn.sigmoid(z_gate)
    u_ext = jnp.concatenate([conv_buf.astype(u.dtype), u], axis=1)
    y = lax.conv_general_dilated(
        u_ext, conv_w[:, None, :].astype(u.dtype), window_strides=(1,), padding='VALID',
        dimension_numbers=('NWC', 'WIO', 'NWC'), feature_group_count=CONV_CH)
    y = y + conv_b.astype(u.dtype)
    y = jax.nn.silu(layernorm(y, ln_g, ln_b))
    return y, u_ext[:, -CONV_BUF:]


def rwkv7_recurrence(S0, r, w, k, v, kk, a):
    def step(S, inp):
        r_t, w_t, k_t, v_t, kk_t, a_t = inp
        sa = jnp.einsum('bhij,bhj->bhi', S, -kk_t)
        S = (S * w_t[:, :, None, :]
             + sa[..., None] * (kk_t * a_t)[:, :, None, :]
             + v_t[..., None] * k_t[:, :, None, :])
        o_t = jnp.einsum('bhij,bhj->bhi', S, r_t)
        return S, o_t
    xs = tuple(jnp.moveaxis(t, 1, 0) for t in (r, w, k, v, kk, a))
    S, o = lax.scan(step, S0, xs)
    return jnp.moveaxis(o, 0, 1), S


def rwkv7_group(zr, zr_prev, S0, mu, w0, w_up, a0, a_up, g_up, k_k, k_a, r_k, gn_g, gn_b):
    f32 = jnp.float32
    zr = zr.astype(f32)
    zs = zr + (zr_prev.astype(f32) - zr) * mu.astype(f32)
    r, k, v, dw, da, dg = jnp.split(zs, RWKV_SPLITS, axis=-1)
    w_raw = -jax.nn.softplus(-(w0.astype(f32) + jnp.tanh(dw) @ w_up.astype(f32))) - 0.5
    decay = jnp.exp(-jnp.exp(w_raw))
    a = jax.nn.sigmoid(a0.astype(f32) + da @ a_up.astype(f32))
    gate = jax.nn.sigmoid(dg) @ g_up.astype(f32)
    B, T = zr.shape[0], zr.shape[1]
    hs = lambda t: t.reshape(B, T, RWKV_HEADS, RWKV_HEAD)
    r, k, v, decay, a = hs(r), hs(k), hs(v), hs(decay), hs(a)
    kk = k * k_k.astype(f32).reshape(RWKV_HEADS, RWKV_HEAD)
    kk = kk / jnp.maximum(jnp.sqrt(jnp.sum(kk * kk, axis=-1, keepdims=True)), 1e-12)
    k = k * (1.0 + (a - 1.0) * k_a.astype(f32).reshape(RWKV_HEADS, RWKV_HEAD))
    o, S = rwkv7_recurrence(S0.astype(f32), r, decay, k, v, kk, a)
    bonus = jnp.sum(r * k * r_k.astype(f32), axis=-1, keepdims=True) * v
    o = head_groupnorm(o, gn_g, gn_b) + bonus
    return o.reshape(B, T, RWKV_DIM) * gate, S


def hybrid_mixer(l, h, h_prev, conv_buf, S0, P):
    w_in = P['w_in'][l]
    z = h @ w_in
    z_val, z_gate, zr = z[..., :CONV_CH], z[..., CONV_CH:2 * CONV_CH], z[..., 2 * CONV_CH:]
    zr_prev_row = h_prev.astype(h.dtype) @ w_in[:, 2 * CONV_CH:]
    zr_prev = jnp.concatenate([zr_prev_row[:, None, :], zr[:, :-1]], axis=1)
    conv_out, new_conv = conformer_conv_group(
        z_val, z_gate, conv_buf, P['conv_w'][l], P['conv_b'][l],
        P['conv_ln_g'][l], P['conv_ln_b'][l])
    rwkv_out, new_S = rwkv7_group(
        zr, zr_prev, S0, P['mu_shift'][l], P['rwkv_w0'][l], P['rwkv_w_up'][l],
        P['rwkv_a0'][l], P['rwkv_a_up'][l], P['rwkv_g_up'][l], P['rwkv_k_k'][l],
        P['rwkv_k_a'][l], P['rwkv_r_k'][l], P['rwkv_gn_g'][l], P['rwkv_gn_b'][l])
    mixed = jnp.concatenate([conv_out, rwkv_out.astype(h.dtype)], axis=-1) @ P['w_out'][l]
    return mixed, new_conv, new_S, h[:, -1]


def mem_cross_attn(h, mk, mv, wq, wo):
    B, T = h.shape[0], h.shape[1]
    q = (h @ wq).reshape(B, T, MEM_HEADS, MEM_HEAD_DIM)
    s = jnp.einsum('bqhd,bkhd->bhqk', q, mk.astype(h.dtype)).astype(jnp.float32)
    p = jax.nn.softmax(s * (MEM_HEAD_DIM ** -0.5), axis=-1).astype(h.dtype)
    o = jnp.einsum('bhqk,bkhd->bqhd', p, mv.astype(h.dtype)).reshape(B, T, D_MODEL)
    return o @ wo


def trunk_layer(l, x, shift_prev, conv_buf, S0, mk, mv, P):
    g = P['norm_g'][l]
    h = rmsnorm(x, g[0])
    m, new_conv, new_S, new_shift = hybrid_mixer(l, h, shift_prev, conv_buf, S0, P)
    x = x + rmsnorm(m, g[1])
    h = rmsnorm(x, g[2])
    x = x + rmsnorm(mem_cross_attn(h, mk, mv, P['w_q'][l], P['w_o'][l]), g[3])
    h = rmsnorm(x, g[4])
    f = jnp.square(jax.nn.relu(h @ P['w_ffn1'][l])) @ P['w_ffn2'][l]
    x = x + rmsnorm(f, g[5])
    return x, new_shift, new_conv, new_S


def setup_inputs(seed: int = 0) -> dict:
    key = jax.random.key(seed)
    ks = jax.random.split(key, 40)
    f32 = jnp.float32
    nrm = lambda k, shape, scale: jax.random.normal(k, shape, f32) * scale
    return {
        "x_prompt": nrm(ks[0], (BATCH, SEQ, D_MODEL), 1.0),
        "x_sample": nrm(ks[1], (DEC_BATCH, DEC_SEQ, D_MODEL), 1.0),
        "cache_mem_k": nrm(ks[2], (DEPTH, DEC_BATCH, N_MEM, MEM_HEADS, MEM_HEAD_DIM), 1.0),
        "cache_mem_v": nrm(ks[3], (DEPTH, DEC_BATCH, N_MEM, MEM_HEADS, MEM_HEAD_DIM), 1.0),
        "state_wkv": nrm(ks[4], (DEPTH, DEC_BATCH, RWKV_HEADS, RWKV_HEAD, RWKV_HEAD), 0.5),
        "state_conv": nrm(ks[5], (DEPTH, DEC_BATCH, CONV_BUF, CONV_CH), 0.5),
        "state_shift": nrm(ks[6], (DEPTH, DEC_BATCH, D_MODEL), 1.0),
        "mem_prompt": nrm(ks[7], (BATCH, N_MEM, D_MODEL), 1.0),
        "w_in": nrm(ks[8], (DEPTH, D_MODEL, IN_COLS), D_MODEL ** -0.5),
        "mu_shift": jax.random.uniform(ks[9], (DEPTH, RWKV_COLS), f32, 0.1, 0.9),
        "conv_w": nrm(ks[10], (DEPTH, CONV_WIDTH, CONV_CH), CONV_WIDTH ** -0.5),
        "conv_b": nrm(ks[11], (DEPTH, CONV_CH), 0.01),
        "conv_ln_g": 1.0 + nrm(ks[12], (DEPTH, CONV_CH), 0.02),
        "conv_ln_b": nrm(ks[13], (DEPTH, CONV_CH), 0.01),
        "rwkv_w0": -1.0 + nrm(ks[14], (DEPTH, RWKV_DIM), 0.5),
        "rwkv_w_up": nrm(ks[15], (DEPTH, DECAY_RANK, RWKV_DIM), 0.1),
        "rwkv_a0": nrm(ks[16], (DEPTH, RWKV_DIM), 0.1),
        "rwkv_a_up": nrm(ks[17], (DEPTH, ICLR_RANK, RWKV_DIM), ICLR_RANK ** -0.5),
        "rwkv_g_up": nrm(ks[18], (DEPTH, GATE_RANK, RWKV_DIM), GATE_RANK ** -0.5),
        "rwkv_k_k": 0.85 + nrm(ks[19], (DEPTH, RWKV_DIM), 0.02),
        "rwkv_k_a": 1.0 + nrm(ks[20], (DEPTH, RWKV_DIM), 0.02),
        "rwkv_r_k": nrm(ks[21], (DEPTH, RWKV_HEADS, RWKV_HEAD), 0.1),
        "rwkv_gn_g": 1.0 + nrm(ks[22], (DEPTH, RWKV_DIM), 0.02),
        "rwkv_gn_b": nrm(ks[23], (DEPTH, RWKV_DIM), 0.01),
        "w_out": nrm(ks[24], (DEPTH, MIX_WIDTH, D_MODEL), MIX_WIDTH ** -0.5),
        "mem_norm_g": 1.0 + nrm(ks[25], (D_MODEL,), 0.02),
        "w_q": nrm(ks[26], (DEPTH, D_MODEL, D_MODEL), D_MODEL ** -0.5),
        "w_k": nrm(ks[27], (DEPTH, D_MODEL, D_MODEL), D_MODEL ** -0.5),
        "w_v": nrm(ks[28], (DEPTH, D_MODEL, D_MODEL), D_MODEL ** -0.5),
        "w_o": nrm(ks[29], (DEPTH, D_MODEL, D_MODEL), D_MODEL ** -0.5),
        "w_ffn1": nrm(ks[30], (DEPTH, D_MODEL, D_FF), D_MODEL ** -0.5),
        "w_ffn2": nrm(ks[31], (DEPTH, D_FF, D_MODEL), D_FF ** -0.5),
        "norm_g": 1.0 + nrm(ks[32], (DEPTH, 6, D_MODEL), 0.02),
    }


def reference(x_prompt, x_sample, cache_mem_k, cache_mem_v, state_wkv, state_conv, state_shift,
              mem_prompt, w_in, mu_shift, conv_w, conv_b, conv_ln_g, conv_ln_b,
              rwkv_w0, rwkv_w_up, rwkv_a0, rwkv_a_up, rwkv_g_up, rwkv_k_k, rwkv_k_a, rwkv_r_k,
              rwkv_gn_g, rwkv_gn_b, w_out, mem_norm_g, w_q, w_k, w_v, w_o, w_ffn1, w_ffn2, norm_g):
    P = dict(w_in=w_in, mu_shift=mu_shift, conv_w=conv_w, conv_b=conv_b, conv_ln_g=conv_ln_g,
             conv_ln_b=conv_ln_b, rwkv_w0=rwkv_w0, rwkv_w_up=rwkv_w_up, rwkv_a0=rwkv_a0,
             rwkv_a_up=rwkv_a_up, rwkv_g_up=rwkv_g_up, rwkv_k_k=rwkv_k_k, rwkv_k_a=rwkv_k_a,
             rwkv_r_k=rwkv_r_k, rwkv_gn_g=rwkv_gn_g, rwkv_gn_b=rwkv_gn_b, w_out=w_out,
             w_q=w_q, w_o=w_o, w_ffn1=w_ffn1, w_ffn2=w_ffn2, norm_g=norm_g)

    Bp = x_prompt.shape[0]
    mem_n = rmsnorm(mem_prompt, mem_norm_g)
    shift0 = jnp.zeros((Bp, D_MODEL), x_prompt.dtype)
    conv0 = jnp.zeros((Bp, CONV_BUF, CONV_CH), x_prompt.dtype)
    wkv0 = jnp.zeros((Bp, RWKV_HEADS, RWKV_HEAD, RWKV_HEAD), jnp.float32)
    x = x_prompt
    mk_p, mv_p, wkv_p, conv_p, shift_p = [], [], [], [], []
    for l in range(DEPTH):
        mk = (mem_n @ w_k[l]).reshape(Bp, N_MEM, MEM_HEADS, MEM_HEAD_DIM)
        mv = (mem_n @ w_v[l]).reshape(Bp, N_MEM, MEM_HEADS, MEM_HEAD_DIM)
        x, s_shift, s_conv, s_wkv = trunk_layer(l, x, shift0, conv0, wkv0, mk, mv, P)
        mk_p.append(mk); mv_p.append(mv); wkv_p.append(s_wkv)
        conv_p.append(s_conv); shift_p.append(s_shift)
    y_prompt = x

    x = x_sample
    wkv_s, conv_s, shift_s = [], [], []
    for l in range(DEPTH):
        x, s_shift, s_conv, s_wkv = trunk_layer(
            l, x, state_shift[l], state_conv[l], state_wkv[l], cache_mem_k[l], cache_mem_v[l], P)
        wkv_s.append(s_wkv); conv_s.append(s_conv); shift_s.append(s_shift)
    y_sample = x

    new_mem_k_prompt = jnp.stack(mk_p)
    new_mem_v_prompt = jnp.stack(mv_p)
    new_wkv_prompt = jnp.stack(wkv_p)
    new_conv_prompt = jnp.stack(conv_p)
    new_shift_prompt = jnp.stack(shift_p)
    new_wkv_sample = jnp.stack(wkv_s)
    new_conv_sample = jnp.stack(conv_s)
    new_shift_sample = jnp.stack(shift_s)
    return (y_prompt, y_sample, new_mem_k_prompt, new_mem_v_prompt, new_wkv_prompt,
            new_conv_prompt, new_shift_prompt, new_wkv_sample, new_conv_sample, new_shift_sample)
```

```python
import functools
import math

import jax
import jax.numpy as jnp
from jax import lax
from jax.experimental import pallas as pl
from jax.experimental.pallas import tpu as pltpu

F32 = jnp.float32
BF16 = jnp.bfloat16

D_MODEL = 1024
CONV_CH = 512
RWKV_DIM = 512
RWKV_HEAD = 64
RWKV_HEADS = 8
CONV_WIDTH = 31
CONV_BUF = CONV_WIDTH - 1
DECAY_RANK = 64
ICLR_RANK = 64
GATE_RANK = 128
RWKV_COLS = 3 * RWKV_DIM + DECAY_RANK + ICLR_RANK + GATE_RANK
N_MEM = 256
MEM_HEADS = 4
MEM_HEAD_DIM = D_MODEL // MEM_HEADS
D_FF = 4 * D_MODEL
RMS_EPS = 1e-6
LN_EPS = 1e-5
GN_EPS = 64e-5

V7X_VMEM_LIMIT_BYTES = 56 * 1024 * 1024

ROW_BLOCK = 512
COL_CHUNK = 1024
CONV_TILE = 256
CONV_HALO_ROW = 2
RWKV_CHUNK = 64
ATTN_TILE = 512
ATTN_SAMPLE_BATCH = 4

_NT = (((1,), (1,)), ((), ()))
_TN = (((0,), (0,)), ((), ()))


def _params(semantics):
    return pltpu.CompilerParams(dimension_semantics=semantics,
                                vmem_limit_bytes=V7X_VMEM_LIMIT_BYTES)


def _dot(a, b):
    return jnp.dot(a, b, preferred_element_type=F32)


def _dot_hi(a, b, dims=None):
    if dims is None:
        dims = (((a.ndim - 1,), (0,)), ((), ()))
    return lax.dot_general(a, b, dims, precision=lax.Precision.HIGHEST,
                           preferred_element_type=F32)


def _split3(x):
    hi = x.astype(BF16)
    r1 = x - hi.astype(F32)
    mid = r1.astype(BF16)
    lo = (r1 - mid.astype(F32)).astype(BF16)
    return hi, mid, lo


def _dot_exact_rhs(x, m01):
    hi, mid, lo = _split3(x)
    return _dot(hi, m01) + _dot(mid, m01) + _dot(lo, m01)


def _dot_exact_lhs(m01, x):
    hi, mid, lo = _split3(x)
    return _dot(m01, hi) + _dot(m01, mid) + _dot(m01, lo)


def _rms(x, g):
    ms = jnp.mean(x * x, axis=-1, keepdims=True)
    return x * lax.rsqrt(ms + RMS_EPS) * g


def _norm_matmul_body(*refs, n_w, normalize, relu2):
    x_ref, g_ref = refs[0], refs[1]
    w_refs = refs[2:2 + n_w]
    o_refs = refs[2 + n_w:]
    x = x_ref[...]
    h = _rms(x, g_ref[...]) if normalize else x
    hb = h.astype(BF16)
    for w_ref, o_ref in zip(w_refs, o_refs):
        n = w_ref.shape[1]
        for c0 in range(0, n, COL_CHUNK):
            c1 = min(n, c0 + COL_CHUNK)
            y = _dot(hb, w_ref[:, c0:c1])
            if relu2:
                y = jnp.square(jnp.maximum(y, 0.0))
            o_ref[:, c0:c1] = y.astype(o_ref.dtype)


def _norm_matmul(x, g, weights, *, out_dtype, normalize=True, relu2=False, name):
    m, k = x.shape
    bm = min(ROW_BLOCK, m)
    assert m % bm == 0
    n_w = len(weights)
    in_specs = [pl.BlockSpec((bm, k), lambda i: (i, 0)),
                pl.BlockSpec((1, k), lambda i: (0, 0))]
    in_specs += [pl.BlockSpec(w.shape, lambda i: (0, 0)) for w in weights]
    out_specs = [pl.BlockSpec((bm, w.shape[1]), lambda i: (i, 0)) for w in weights]
    out_shape = [jax.ShapeDtypeStruct((m, w.shape[1]), out_dtype) for w in weights]
    return pl.pallas_call(
        functools.partial(_norm_matmul_body, n_w=n_w, normalize=normalize, relu2=relu2),
        grid=(m // bm,), in_specs=in_specs, out_specs=out_specs, out_shape=out_shape,
        compiler_params=_params(("parallel",)), name=name,
    )(x, g, *weights)


def _matmul_norm_res_body(y_ref, w_ref, x_ref, g_ref, o_ref):
    m = _dot(y_ref[...], w_ref[...])
    o_ref[...] = x_ref[...] + _rms(m, g_ref[...])


def _matmul_norm_res(y, w, x, g, *, name):
    m, k = y.shape
    n = w.shape[1]
    bm = min(ROW_BLOCK, m)
    assert m % bm == 0
    return pl.pallas_call(
        _matmul_norm_res_body,
        grid=(m // bm,),
        in_specs=[pl.BlockSpec((bm, k), lambda i: (i, 0)),
                  pl.BlockSpec((k, n), lambda i: (0, 0)),
                  pl.BlockSpec((bm, n), lambda i: (i, 0)),
                  pl.BlockSpec((1, n), lambda i: (0, 0))],
        out_specs=pl.BlockSpec((bm, n), lambda i: (i, 0)),
        out_shape=jax.ShapeDtypeStruct((m, n), F32),
        compiler_params=_params(("parallel",)), name=name,
    )(y, w, x, g)


def _rms_rows_body(x_ref, g_ref, o_ref):
    o_ref[...] = _rms(x_ref[...], g_ref[...])


def _rms_rows(x, g):
    m, k = x.shape
    return pl.pallas_call(
        _rms_rows_body, grid=(1,),
        in_specs=[pl.BlockSpec((m, k), lambda i: (0, 0)),
                  pl.BlockSpec((1, k), lambda i: (0, 0))],
        out_specs=pl.BlockSpec((m, k), lambda i: (0, 0)),
        out_shape=jax.ShapeDtypeStruct((m, k), F32),
        compiler_params=_params(("arbitrary",)), name="rms_rows",
    )(x, g)


def _conv_body(z_ref, buf_ref, w_ref, b_ref, lg_ref, lb_ref, y_ref, nc_ref, ext_ref, *, tt):
    h0 = CONV_HALO_ROW
    cur = h0 + CONV_BUF

    @pl.when(pl.program_id(1) == 0)
    def _():
        ext_ref[h0:cur, :] = buf_ref[0]

    z = z_ref[...]
    u = z[:, :CONV_CH] * jax.nn.sigmoid(z[:, CONV_CH:])
    ext_ref[cur:cur + tt, :] = u
    acc = ext_ref[h0:h0 + tt, :] * w_ref[0:1, :]
    for k in range(1, CONV_WIDTH):
        acc = acc + ext_ref[h0 + k:h0 + k + tt, :] * w_ref[k:k + 1, :]
    y = acc + b_ref[...]
    mu = jnp.mean(y, axis=-1, keepdims=True)
    var = jnp.mean(jnp.square(y - mu), axis=-1, keepdims=True)
    yn = (y - mu) * lax.rsqrt(var + LN_EPS) * lg_ref[...] + lb_ref[...]
    y_ref[...] = (yn * jax.nn.sigmoid(yn)).astype(y_ref.dtype)
    halo = ext_ref[tt + h0:tt + cur, :]
    ext_ref[h0:cur, :] = halo
    nc_ref[0] = halo


def _conv_group(zc, row0, batch, seq, tt, conv_buf, w, b, lg, lb, *, out_dtype, name):
    nt = seq // tt
    base = row0 // tt
    assert seq % tt == 0 and row0 % tt == 0
    return pl.pallas_call(
        functools.partial(_conv_body, tt=tt),
        grid=(batch, nt),
        in_specs=[pl.BlockSpec((tt, 2 * CONV_CH), lambda bi, j: (base + bi * nt + j, 0)),
                  pl.BlockSpec((1, CONV_BUF, CONV_CH), lambda bi, j: (bi, 0, 0)),
                  pl.BlockSpec((CONV_WIDTH, CONV_CH), lambda bi, j: (0, 0)),
                  pl.BlockSpec((1, CONV_CH), lambda bi, j: (0, 0)),
                  pl.BlockSpec((1, CONV_CH), lambda bi, j: (0, 0)),
                  pl.BlockSpec((1, CONV_CH), lambda bi, j: (0, 0))],
        out_specs=[pl.BlockSpec((tt, CONV_CH), lambda bi, j: (bi * nt + j, 0)),
                   pl.BlockSpec((1, CONV_BUF, CONV_CH), lambda bi, j: (bi, 0, 0))],
        out_shape=[jax.ShapeDtypeStruct((batch * seq, CONV_CH), out_dtype),
                   jax.ShapeDtypeStruct((batch, CONV_BUF, CONV_CH), F32)],
        scratch_shapes=[pltpu.VMEM((CONV_HALO_ROW + CONV_BUF + tt, CONV_CH), F32)],
        compiler_params=_params(("parallel", "arbitrary")), name=name,
    )(zc, conv_buf, w, b, lg, lb)


def _rwkv_body(zr_ref, zp0_ref, s0_ref, mu_ref, w0_ref, wup_ref, a0_ref, aup_ref, gup_ref,
               kk_ref, ka_ref, rk_ref, gng_ref, gnb_ref, seg_ref,
               o_ref, s_out_ref, s_scr, carry_scr, *, chunk):
    c = chunk

    @pl.when(pl.program_id(1) == 0)
    def _():
        s_scr[...] = s0_ref[0]
        carry_scr[...] = zp0_ref[0]

    zr = zr_ref[...]
    row = lax.broadcasted_iota(jnp.int32, (c, 1), 0)
    prev = jnp.where(row == 0, carry_scr[...], pltpu.roll(zr, 1, axis=0))
    carry_scr[...] = zr[c - 1:c, :]
    zs = zr + (prev - zr) * mu_ref[...]

    d = RWKV_DIM
    r = zs[:, 0:d]
    k = zs[:, d:2 * d]
    v = zs[:, 2 * d:3 * d]
    dw = zs[:, 3 * d:3 * d + DECAY_RANK]
    da = zs[:, 3 * d + DECAY_RANK:3 * d + DECAY_RANK + ICLR_RANK]
    dg = zs[:, 3 * d + DECAY_RANK + ICLR_RANK:]

    lw = w0_ref[...] + _dot(jnp.tanh(dw).astype(BF16), wup_ref[...])
    w_raw = jnp.minimum(lw, 0.0) - jnp.log(1.0 + jnp.exp(-jnp.abs(lw))) - 0.5
    logw = -jnp.exp(w_raw)
    a = jax.nn.sigmoid(a0_ref[...] + _dot(da.astype(BF16), aup_ref[...]))
    gate = _dot(jax.nn.sigmoid(dg).astype(BF16), gup_ref[...])

    seg = seg_ref[...]
    kk = k * kk_ref[...]
    kk = kk / jnp.maximum(jnp.sqrt(_dot_exact_rhs(kk * kk, seg)), 1e-12)
    kmod = k * (1.0 + (a - 1.0) * ka_ref[...])
    bonus = _dot_exact_rhs(r * kmod * rk_ref[...], seg) * v

    ri = lax.broadcasted_iota(jnp.int32, (c, c), 0)
    ci = lax.broadcasted_iota(jnp.int32, (c, c), 1)
    incl = ri >= ci
    strict = ri > ci
    cl = _dot_exact_lhs(incl.astype(BF16), logw)
    g_incl = jnp.exp(cl)
    g_inv = jnp.exp(-cl)
    a_t = -kk * jnp.exp(cl - logw)
    b_t = kk * a * g_inv
    k_t = kmod * g_inv
    r_t = r * g_incl
    g_last = g_incl[c - 1:c, :]

    eye = (ri == ci).astype(F32)
    n_double = max(0, int(math.ceil(math.log2(c))) - 1)
    outs = []
    for h in range(RWKV_HEADS):
        sl = slice(h * RWKV_HEAD, (h + 1) * RWKV_HEAD)
        ah, bh, kh, rh, vh = a_t[:, sl], b_t[:, sl], k_t[:, sl], r_t[:, sl], v[:, sl]
        s0 = s_scr[h]
        a_ab = jnp.where(strict, _dot_hi(ah, bh, _NT), 0.0)
        a_ak = jnp.where(strict, _dot_hi(ah, kh, _NT), 0.0)
        a_rb = jnp.where(incl, _dot_hi(rh, bh, _NT), 0.0)
        a_rk = jnp.where(incl, _dot_hi(rh, kh, _NT), 0.0)
        rhs = _dot_hi(ah, s0, _NT) + _dot_hi(a_ak, vh)
        inv = eye + a_ab
        npow = a_ab
        for _ in range(n_double):
            npow = _dot_hi(npow, npow)
            inv = inv + _dot_hi(inv, npow)
        u = _dot_hi(inv, rhs)
        outs.append(_dot_hi(rh, s0, _NT) + _dot_hi(a_rb, u) + _dot_hi(a_rk, vh))
        s_scr[h] = (s0 + _dot_hi(u, bh, _TN) + _dot_hi(vh, kh, _TN)) * g_last[:, sl]
    o = jnp.concatenate(outs, axis=1)

    inv_n = 1.0 / RWKV_HEAD
    mean = _dot_exact_rhs(o, seg) * inv_n
    cen = o - mean
    var = _dot_exact_rhs(cen * cen, seg) * inv_n
    y = cen * lax.rsqrt(var + GN_EPS) * gng_ref[...] + gnb_ref[...] + bonus
    o_ref[...] = (y * gate).astype(o_ref.dtype)
    s_out_ref[0] = s_scr[...]


def _rwkv_group(zr, row0, batch, seq, chunk, zp0, s0, p, seg, *, out_dtype, name):
    nc = seq // chunk
    base = row0 // chunk
    assert seq % chunk == 0 and row0 % chunk == 0
    vec = lambda n: pl.BlockSpec((1, n), lambda bi, j: (0, 0))
    mat = lambda a: pl.BlockSpec(a.shape, lambda bi, j: (0, 0))
    st = (1, RWKV_HEADS, RWKV_HEAD, RWKV_HEAD)
    return pl.pallas_call(
        functools.partial(_rwkv_body, chunk=chunk),
        grid=(batch, nc),
        in_specs=[pl.BlockSpec((chunk, RWKV_COLS), lambda bi, j: (base + bi * nc + j, 0)),
                  pl.BlockSpec((1, 1, RWKV_COLS), lambda bi, j: (bi, 0, 0)),
                  pl.BlockSpec(st, lambda bi, j: (bi, 0, 0, 0)),
                  vec(RWKV_COLS), vec(RWKV_DIM), mat(p["w_up"]), vec(RWKV_DIM), mat(p["a_up"]),
                  mat(p["g_up"]), vec(RWKV_DIM), vec(RWKV_DIM), vec(RWKV_DIM), vec(RWKV_DIM),
                  vec(RWKV_DIM), mat(seg)],
        out_specs=[pl.BlockSpec((chunk, RWKV_DIM), lambda bi, j: (bi * nc + j, 0)),
                   pl.BlockSpec(st, lambda bi, j: (bi, 0, 0, 0))],
        out_shape=[jax.ShapeDtypeStruct((batch * seq, RWKV_DIM), out_dtype),
                   jax.ShapeDtypeStruct((batch,) + st[1:], F32)],
        scratch_shapes=[pltpu.VMEM(st[1:], F32), pltpu.VMEM((1, RWKV_COLS), F32)],
        compiler_params=_params(("parallel", "arbitrary")), name=name,
    )(zr, zp0, s0, p["mu"], p["w0"], p["w_up"], p["a0"], p["a_up"], p["g_up"],
      p["k_k"], p["k_a"], p["r_k"], p["gn_g"], p["gn_b"], seg)


def _attn_body(q_ref, k_ref, v_ref, o_ref, *, bb, tq):
    scale = MEM_HEAD_DIM ** -0.5
    for i in range(bb):
        rows = slice(i * tq, (i + 1) * tq)
        for h in range(MEM_HEADS):
            sl = slice(h * MEM_HEAD_DIM, (h + 1) * MEM_HEAD_DIM)
            kh = k_ref[i, :, sl].astype(BF16)
            vh = v_ref[i, :, sl].astype(BF16)
            s = lax.dot_general(q_ref[rows, sl], kh, _NT, preferred_element_type=F32) * scale
            e = jnp.exp(s - jnp.max(s, axis=-1, keepdims=True))
            p = e / jnp.sum(e, axis=-1, keepdims=True)
            o_ref[rows, sl] = _dot(p.astype(BF16), vh).astype(o_ref.dtype)


def _attn_group(q, row0, batch, seq, tq, bb, mk, mv, *, name):
    nt = seq // tq
    rows = bb * tq
    base = row0 // rows
    assert seq % tq == 0 and batch % bb == 0 and row0 % rows == 0 and (bb == 1 or nt == 1)
    kv_spec = pl.BlockSpec((bb, N_MEM, D_MODEL), lambda bi, j: (bi, 0, 0))
    return pl.pallas_call(
        functools.partial(_attn_body, bb=bb, tq=tq),
        grid=(batch // bb, nt),
        in_specs=[pl.BlockSpec((rows, D_MODEL), lambda bi, j: (base + bi * nt + j, 0)),
                  kv_spec, kv_spec],
        out_specs=pl.BlockSpec((rows, D_MODEL), lambda bi, j: (bi * nt + j, 0)),
        out_shape=jax.ShapeDtypeStruct((batch * seq, D_MODEL), BF16),
        compiler_params=_params(("parallel", "arbitrary")), name=name,
    )(q, mk, mv)


def kernel(x_prompt, x_sample, cache_mem_k, cache_mem_v, state_wkv, state_conv, state_shift,
           mem_prompt, w_in, mu_shift, conv_w, conv_b, conv_ln_g, conv_ln_b, rwkv_w0, rwkv_w_up,
           rwkv_a0, rwkv_a_up, rwkv_g_up, rwkv_k_k, rwkv_k_a, rwkv_r_k, rwkv_gn_g, rwkv_gn_b,
           w_out, mem_norm_g, w_q, w_k, w_v, w_o, w_ffn1, w_ffn2, norm_g):
    bp, tp, _ = x_prompt.shape
    bs, ts, _ = x_sample.shape
    depth = w_in.shape[0]
    n_p = bp * tp
    row = lambda a: a.reshape(1, -1)

    x = jnp.concatenate([x_prompt.reshape(n_p, D_MODEL), x_sample.reshape(bs * ts, D_MODEL)], axis=0)
    mem2d = mem_prompt.reshape(bp * N_MEM, D_MODEL)
    head_id = jnp.arange(RWKV_DIM) // RWKV_HEAD
    seg = (head_id[:, None] == head_id[None, :]).astype(BF16)
    zero_conv = jnp.zeros((bp, CONV_BUF, CONV_CH), F32)
    zero_prev = jnp.zeros((bp, 1, RWKV_COLS), F32)
    zero_wkv = jnp.zeros((bp, RWKV_HEADS, RWKV_HEAD, RWKV_HEAD), F32)

    mk_p, mv_p, wkv_p, conv_p, shift_p, wkv_s, conv_s, shift_s = ([] for _ in range(8))
    for l in range(depth):
        g = norm_g[l]
        w_in_b = w_in[l].astype(BF16)
        w_in_conv, w_in_rwkv = w_in_b[:, :2 * CONV_CH], w_in_b[:, 2 * CONV_CH:]
        rp = dict(mu=row(mu_shift[l]), w0=row(rwkv_w0[l]), w_up=rwkv_w_up[l].astype(BF16),
                  a0=row(rwkv_a0[l]), a_up=rwkv_a_up[l].astype(BF16),
                  g_up=rwkv_g_up[l].astype(BF16), k_k=row(rwkv_k_k[l]), k_a=row(rwkv_k_a[l]),
                  r_k=row(rwkv_r_k[l]), gn_g=row(rwkv_gn_g[l]), gn_b=row(rwkv_gn_b[l]))
        cw, cb = conv_w[l], row(conv_b[l])
        clg, clb = row(conv_ln_g[l]), row(conv_ln_b[l])

        last = jnp.concatenate([x[:n_p].reshape(bp, tp, D_MODEL)[:, -1],
                                x[n_p:].reshape(bs, ts, D_MODEL)[:, -1]], axis=0)
        shift = _rms_rows(last, g[0:1])
        shift_p.append(shift[:bp])
        shift_s.append(shift[bp:])

        zc, zr = _norm_matmul(x, g[0:1], [w_in_conv, w_in_rwkv], out_dtype=F32, name="in_proj")
        (zp_s,) = _norm_matmul(state_shift[l], g[0:1], [w_in_rwkv], out_dtype=F32,
                               normalize=False, name="prev_proj")

        cv_p, nc_p = _conv_group(zc, 0, bp, tp, min(CONV_TILE, tp), zero_conv, cw, cb, clg, clb,
                                 out_dtype=BF16, name="conv_prompt")
        cv_s, nc_s = _conv_group(zc, n_p, bs, ts, ts, state_conv[l], cw, cb, clg, clb,
                                 out_dtype=F32, name="conv_sample")
        rw_p, s_p = _rwkv_group(zr, 0, bp, tp, min(RWKV_CHUNK, tp), zero_prev, zero_wkv, rp, seg,
                                out_dtype=BF16, name="rwkv_prompt")
        rw_s, s_s = _rwkv_group(zr, n_p, bs, ts, ts, zp_s[:, None, :], state_wkv[l], rp, seg,
                                out_dtype=F32, name="rwkv_sample")
        conv_p.append(nc_p)
        conv_s.append(nc_s)
        wkv_p.append(s_p)
        wkv_s.append(s_s)
        mixed = jnp.concatenate(
            [jnp.concatenate([cv_p, rw_p], axis=1),
             jnp.concatenate([cv_s, rw_s], axis=1).astype(BF16)], axis=0)
        x = _matmul_norm_res(mixed, w_out[l].astype(BF16), x, g[1:2], name="out_proj")

        (q,) = _norm_matmul(x, g[2:3], [w_q[l].astype(BF16)], out_dtype=BF16, name="q_proj")
        mk, mv = _norm_matmul(mem2d, row(mem_norm_g), [w_k[l].astype(BF16), w_v[l].astype(BF16)],
                              out_dtype=F32, name="mem_kv_proj")
        mk_p.append(mk.reshape(bp, N_MEM, MEM_HEADS, MEM_HEAD_DIM))
        mv_p.append(mv.reshape(bp, N_MEM, MEM_HEADS, MEM_HEAD_DIM))
        o_p = _attn_group(q, 0, bp, tp, min(ATTN_TILE, tp), 1, mk.reshape(bp, N_MEM, D_MODEL),
                          mv.reshape(bp, N_MEM, D_MODEL), name="attn_prompt")
        o_s = _attn_group(q, n_p, bs, ts, ts, ATTN_SAMPLE_BATCH,
                          cache_mem_k[l].reshape(bs, N_MEM, D_MODEL),
                          cache_mem_v[l].reshape(bs, N_MEM, D_MODEL), name="attn_sample")
        x = _matmul_norm_res(jnp.concatenate([o_p, o_s], axis=0), w_o[l].astype(BF16), x, g[3:4],
                             name="attn_out_proj")

        (f,) = _norm_matmul(x, g[4:5], [w_ffn1[l].astype(BF16)], out_dtype=BF16, relu2=True,
                            name="ffn_up")
        x = _matmul_norm_res(f, w_ffn2[l].astype(BF16), x, g[5:6], name="ffn_down")

    y_prompt = x[:n_p].reshape(bp, tp, D_MODEL)
    y_sample = x[n_p:].reshape(bs, ts, D_MODEL)
    return (y_prompt, y_sample, jnp.stack(mk_p), jnp.stack(mv_p), jnp.stack(wkv_p),
            jnp.stack(conv_p), jnp.stack(shift_p), jnp.stack(wkv_s), jnp.stack(conv_s),
            jnp.stack(shift_s))
```

```python
import functools
import math

import jax
import jax.numpy as jnp
from jax import lax
from jax.experimental import pallas as pl
from jax.experimental.pallas import tpu as pltpu

F32 = jnp.float32
BF16 = jnp.bfloat16

D_MODEL = 1024
CONV_CH = 512
RWKV_DIM = 512
RWKV_HEAD = 64
RWKV_HEADS = 8
CONV_WIDTH = 31
CONV_BUF = CONV_WIDTH - 1
DECAY_RANK = 64
ICLR_RANK = 64
GATE_RANK = 128
RWKV_COLS = 3 * RWKV_DIM + DECAY_RANK + ICLR_RANK + GATE_RANK
N_MEM = 256
MEM_HEADS = 4
MEM_HEAD_DIM = D_MODEL // MEM_HEADS
D_FF = 4 * D_MODEL
RMS_EPS = 1e-6
LN_EPS = 1e-5
GN_EPS = 64e-5

V7X_LANES = 128
V7X_VMEM_LIMIT_BYTES = 56 * 1024 * 1024

ROW_BLOCK = 512
COL_CHUNK = 1024
CONV_TILE = 256
CONV_HALO_ROW = 2
RWKV_ROWS = 64
INV_BASE = 8
ATTN_TILE = 512
ATTN_SAMPLE_BATCH = 4

_NT = (((1,), (1,)), ((), ()))
_TN = (((0,), (0,)), ((), ()))


def _params(semantics):
    return pltpu.CompilerParams(dimension_semantics=semantics,
                                vmem_limit_bytes=V7X_VMEM_LIMIT_BYTES)


def _dot(a, b):
    return jnp.dot(a, b, preferred_element_type=F32)


def _mm(a, b, dims=None):
    if dims is None:
        dims = (((a.ndim - 1,), (0,)), ((), ()))
    return lax.dot_general(a.astype(BF16), b.astype(BF16), dims, preferred_element_type=F32)


def _split2(x):
    hi = x.astype(BF16)
    lo = (x - hi.astype(F32)).astype(BF16)
    return hi, lo


def _split3(x):
    hi = x.astype(BF16)
    r1 = x - hi.astype(F32)
    mid = r1.astype(BF16)
    lo = (r1 - mid.astype(F32)).astype(BF16)
    return hi, mid, lo


def _rms(x, g):
    ms = jnp.mean(x * x, axis=-1, keepdims=True)
    return x * lax.rsqrt(ms + RMS_EPS) * g


def _drop_carried(body, n_in, n_carried):
    def wrapped(*refs):
        return body(*refs[:n_in], *refs[n_in + n_carried:])
    return wrapped


def _call_with_carried(body, *, grid, in_specs, inputs, out_specs, out_shape, carried,
                       scratch_shapes=(), semantics, name):
    n_in = len(inputs)
    extra, aliases = [], {}
    for out_idx, buf in sorted(carried.items()):
        aliases[n_in + len(extra)] = out_idx
        extra.append(buf)
    return pl.pallas_call(
        _drop_carried(body, n_in, len(extra)),
        grid=grid,
        in_specs=list(in_specs) + [pl.BlockSpec(memory_space=pl.ANY)] * len(extra),
        out_specs=out_specs, out_shape=out_shape, scratch_shapes=list(scratch_shapes),
        input_output_aliases=aliases, compiler_params=_params(semantics), name=name,
    )(*inputs, *extra)


def _norm_matmul_body(*refs, n_w, normalize, relu2):
    x_ref, g_ref = refs[0], refs[1]
    w_refs = refs[2:2 + n_w]
    o_refs = refs[2 + n_w:]
    x = x_ref[...]
    h = _rms(x, g_ref[...]) if normalize else x
    hb = h.astype(BF16)
    for w_ref, o_ref in zip(w_refs, o_refs):
        n = w_ref.shape[1]
        for c0 in range(0, n, COL_CHUNK):
            c1 = min(n, c0 + COL_CHUNK)
            y = _dot(hb, w_ref[:, c0:c1])
            if relu2:
                y = jnp.square(jnp.maximum(y, 0.0))
            o_ref[:, c0:c1] = y.astype(o_ref.dtype)


def _norm_matmul(x, g, weights, *, out_dtype, normalize=True, relu2=False, name):
    m, k = x.shape
    bm = min(ROW_BLOCK, m)
    assert m % bm == 0
    n_w = len(weights)
    in_specs = [pl.BlockSpec((bm, k), lambda i: (i, 0)),
                pl.BlockSpec((1, k), lambda i: (0, 0))]
    in_specs += [pl.BlockSpec(w.shape, lambda i: (0, 0)) for w in weights]
    out_specs = [pl.BlockSpec((bm, w.shape[1]), lambda i: (i, 0)) for w in weights]
    out_shape = [jax.ShapeDtypeStruct((m, w.shape[1]), out_dtype) for w in weights]
    return pl.pallas_call(
        functools.partial(_norm_matmul_body, n_w=n_w, normalize=normalize, relu2=relu2),
        grid=(m // bm,), in_specs=in_specs, out_specs=out_specs, out_shape=out_shape,
        compiler_params=_params(("parallel",)), name=name,
    )(x, g, *weights)


def _matmul_norm_res_body(y_ref, w_ref, x_ref, g_ref, o_ref):
    m = _dot(y_ref[...], w_ref[...])
    o_ref[...] = x_ref[...] + _rms(m, g_ref[...])


def _matmul_norm_res(y, w, x, g, *, name):
    m, k = y.shape
    n = w.shape[1]
    bm = min(ROW_BLOCK, m)
    assert m % bm == 0
    return pl.pallas_call(
        _matmul_norm_res_body,
        grid=(m // bm,),
        in_specs=[pl.BlockSpec((bm, k), lambda i: (i, 0)),
                  pl.BlockSpec((k, n), lambda i: (0, 0)),
                  pl.BlockSpec((bm, n), lambda i: (i, 0)),
                  pl.BlockSpec((1, n), lambda i: (0, 0))],
        out_specs=pl.BlockSpec((bm, n), lambda i: (i, 0)),
        out_shape=jax.ShapeDtypeStruct((m, n), F32),
        compiler_params=_params(("parallel",)), name=name,
    )(y, w, x, g)


def _rms_rows_body(x_ref, g_ref, o_ref):
    o_ref[...] = _rms(x_ref[...], g_ref[...])


def _rms_rows(x, g):
    m, k = x.shape
    return pl.pallas_call(
        _rms_rows_body, grid=(1,),
        in_specs=[pl.BlockSpec((m, k), lambda i: (0, 0)),
                  pl.BlockSpec((1, k), lambda i: (0, 0))],
        out_specs=pl.BlockSpec((m, k), lambda i: (0, 0)),
        out_shape=jax.ShapeDtypeStruct((m, k), F32),
        compiler_params=_params(("arbitrary",)), name="rms_rows",
    )(x, g)


def _conv_body(z_ref, buf_ref, w_ref, b_ref, lg_ref, lb_ref, y_ref, nc_ref, ext_ref, *, nb, tt):
    h0 = CONV_HALO_ROW
    cur = h0 + CONV_BUF
    for i in range(nb):
        rows = slice(i * tt, (i + 1) * tt)
        if nb == 1:
            @pl.when(pl.program_id(1) == 0)
            def _():
                ext_ref[h0:cur, :] = buf_ref[0]
        else:
            ext_ref[h0:cur, :] = buf_ref[i]
        z = z_ref[rows, :]
        u = z[:, :CONV_CH] * jax.nn.sigmoid(z[:, CONV_CH:])
        ext_ref[cur:cur + tt, :] = u
        acc = ext_ref[h0:h0 + tt, :] * w_ref[0:1, :]
        for k in range(1, CONV_WIDTH):
            acc = acc + ext_ref[h0 + k:h0 + k + tt, :] * w_ref[k:k + 1, :]
        y = acc + b_ref[...]
        mu = jnp.mean(y, axis=-1, keepdims=True)
        var = jnp.mean(jnp.square(y - mu), axis=-1, keepdims=True)
        yn = (y - mu) * lax.rsqrt(var + LN_EPS) * lg_ref[...] + lb_ref[...]
        y_ref[rows, :] = (yn * jax.nn.sigmoid(yn)).astype(y_ref.dtype)
        halo = ext_ref[tt + h0:tt + cur, :]
        if nb == 1:
            ext_ref[h0:cur, :] = halo
        nc_ref[i] = halo


def _conv_group(zc, row0, batch, seq, nb, tt, conv_buf, w, b, lg, lb, *, layer, depth, mixed,
                state, name):
    nt = seq // tt
    rows = nb * tt
    base = row0 // rows
    assert seq % tt == 0 and batch % nb == 0 and row0 % rows == 0 and (nb == 1 or nt == 1)
    const = lambda bi, j: (0, 0)
    carried = {}
    if mixed is not None:
        carried[0] = mixed
    if state is not None:
        carried[1] = state
    return _call_with_carried(
        functools.partial(_conv_body, nb=nb, tt=tt),
        grid=(batch // nb, nt),
        in_specs=[pl.BlockSpec((rows, 2 * CONV_CH), lambda bi, j: (base + bi * nt + j, 0)),
                  pl.BlockSpec((nb, CONV_BUF, CONV_CH), lambda bi, j: (bi, 0, 0)),
                  pl.BlockSpec((CONV_WIDTH, CONV_CH), const),
                  pl.BlockSpec((1, CONV_CH), const), pl.BlockSpec((1, CONV_CH), const),
                  pl.BlockSpec((1, CONV_CH), const)],
        inputs=[zc, conv_buf, w, b, lg, lb],
        out_specs=[pl.BlockSpec((rows, CONV_CH), lambda bi, j: (base + bi * nt + j, 0)),
                   pl.BlockSpec((None, nb, CONV_BUF, CONV_CH), lambda bi, j: (layer, bi, 0, 0))],
        out_shape=[jax.ShapeDtypeStruct((zc.shape[0], CONV_CH + RWKV_DIM), BF16),
                   jax.ShapeDtypeStruct((depth, batch, CONV_BUF, CONV_CH), F32)],
        carried=carried,
        scratch_shapes=[pltpu.VMEM((CONV_HALO_ROW + CONV_BUF + tt, CONV_CH), F32)],
        semantics=("parallel", "arbitrary"), name=name)


def _head_sums(xs, seg):
    rows = xs[0].shape[0]
    groups = RWKV_DIM // V7X_LANES
    stacked = jnp.concatenate(
        [x[:, g * V7X_LANES:(g + 1) * V7X_LANES] for x in xs for g in range(groups)], axis=0)
    hi, lo = _split2(stacked)
    y = _dot(hi, seg) + _dot(lo, seg)
    outs = []
    for i in range(len(xs)):
        parts = [y[(i * groups + g) * rows:(i * groups + g + 1) * rows] for g in range(groups)]
        outs.append(jnp.concatenate(parts, axis=1))
    return outs


def _rwkv_body(zr_ref, zp0_ref, s0_ref, mu_ref, w0_ref, wup_ref, a0_ref, aup_ref, gup_ref,
               kk_ref, ka_ref, rk_ref, gng_ref, gnb_ref, seg_ref,
               o_ref, s_out_ref, s_scr, carry_scr, *, nb, tlen):
    rows = nb * tlen
    shift = int(math.log2(tlen))
    assert 1 << shift == tlen

    @pl.when(pl.program_id(1) == 0)
    def _():
        s_scr[...] = s0_ref[...]
        carry_scr[...] = zp0_ref[...]

    zr = zr_ref[...]
    t_idx = lax.broadcasted_iota(jnp.int32, (rows, 1), 0) & (tlen - 1)
    if nb == 1:
        carry_rows = carry_scr[0]
    else:
        carry_rows = jnp.concatenate(
            [jnp.broadcast_to(carry_scr[i], (tlen, RWKV_COLS)) for i in range(nb)], axis=0)
    prev = jnp.where(t_idx == 0, carry_rows, pltpu.roll(zr, 1, axis=0))
    for i in range(nb):
        carry_scr[i] = zr[(i + 1) * tlen - 1:(i + 1) * tlen, :]
    zs = zr + (prev - zr) * mu_ref[...]

    d = RWKV_DIM
    r = zs[:, 0:d]
    k = zs[:, d:2 * d]
    v = zs[:, 2 * d:3 * d]
    dw = zs[:, 3 * d:3 * d + DECAY_RANK]
    da = zs[:, 3 * d + DECAY_RANK:3 * d + DECAY_RANK + ICLR_RANK]
    dg = zs[:, 3 * d + DECAY_RANK + ICLR_RANK:]

    lw = w0_ref[...] + _mm(jnp.tanh(dw), wup_ref[...])
    w_raw = jnp.minimum(lw, 0.0) - jnp.log(1.0 + jnp.exp(-jnp.abs(lw))) - 0.5
    logw = -jnp.exp(w_raw)
    a = jax.nn.sigmoid(a0_ref[...] + _mm(da, aup_ref[...]))
    gate = _mm(jax.nn.sigmoid(dg), gup_ref[...])

    seg = seg_ref[...]
    kk = k * kk_ref[...]
    kmod = k * (1.0 + (a - 1.0) * ka_ref[...])
    kk_sq, bonus_dot = _head_sums([kk * kk, r * kmod * rk_ref[...]], seg)
    kk = kk / jnp.maximum(jnp.sqrt(kk_sq), 1e-12)
    bonus = bonus_dot * v

    ri = lax.broadcasted_iota(jnp.int32, (rows, rows), 0)
    ci = lax.broadcasted_iota(jnp.int32, (rows, rows), 1)
    same_seq = (ri >> shift) == (ci >> shift)
    incl = same_seq & (ri >= ci)
    strict = same_seq & (ri > ci)
    tri = incl.astype(BF16)
    w_hi, w_mid, w_lo = _split3(logw)
    cl = _dot(tri, w_hi) + _dot(tri, w_mid) + _dot(tri, w_lo)
    g_incl = jnp.exp(cl)
    g_inv = jnp.exp(-cl)
    a_t = -kk * jnp.exp(cl - logw)
    b_t = kk * a * g_inv
    k_t = kmod * g_inv
    r_t = r * g_incl

    eye = (ri == ci).astype(F32)
    base_shift = min(shift, int(math.log2(INV_BASE)))
    base_blk = (ri >> base_shift) == (ci >> base_shift)
    level_blks = [((ri >> (lv + 1)) == (ci >> (lv + 1))) & ((ri >> lv) != (ci >> lv))
                  for lv in range(base_shift, shift)]
    heads = range(RWKV_HEADS)
    seqs = [slice(i * tlen, (i + 1) * tlen) for i in range(nb)]
    lanes = [slice(h * RWKV_HEAD, (h + 1) * RWKV_HEAD) for h in heads]
    ah = [a_t[:, sl] for sl in lanes]
    bh = [b_t[:, sl] for sl in lanes]
    kh = [k_t[:, sl] for sl in lanes]
    rh = [r_t[:, sl] for sl in lanes]
    vh = [v[:, sl] for sl in lanes]
    ar = [jnp.concatenate([ah[h], rh[h]], axis=0) for h in heads]
    gb = [_mm(ar[h], bh[h], _NT) for h in heads]
    gk = [_mm(ar[h], kh[h], _NT) for h in heads]
    a_ab = [jnp.where(strict, gb[h][:rows], 0.0) for h in heads]
    a_rb = [jnp.where(incl, gb[h][rows:], 0.0) for h in heads]
    a_ak = [jnp.where(strict, gk[h][:rows], 0.0) for h in heads]
    a_rk = [jnp.where(incl, gk[h][rows:], 0.0) for h in heads]
    s_old = [[s_scr[i, h] for h in heads] for i in range(nb)]
    s_terms = [[_mm(jnp.concatenate([ah[h][rs], rh[h][rs]], axis=0), s_old[i][h], _NT)
                for i, rs in enumerate(seqs)] for h in heads]
    a_s0 = [jnp.concatenate([x[:tlen] for x in s_terms[h]], axis=0) for h in heads]
    r_s0 = [jnp.concatenate([x[tlen:] for x in s_terms[h]], axis=0) for h in heads]
    av = [_mm(jnp.concatenate([a_ak[h], a_rk[h]], axis=0), vh[h]) for h in heads]
    rhs = [a_s0[h] + av[h][:rows] for h in heads]
    npow = [jnp.where(base_blk, a_ab[h], 0.0) for h in heads]
    inv = [eye + npow[h] for h in heads]
    for _ in range(max(0, base_shift - 1)):
        npow = [_mm(npow[h], npow[h]) for h in heads]
        inv = [inv[h] + _mm(inv[h], npow[h]) for h in heads]
    for off_blk in level_blks:
        cross = [_mm(jnp.where(off_blk, a_ab[h], 0.0), inv[h]) for h in heads]
        inv = [inv[h] + _mm(inv[h], cross[h]) for h in heads]
    u = [_mm(inv[h], rhs[h]) for h in heads]
    outs = [r_s0[h] + _mm(a_rb[h], u[h]) + av[h][rows:] for h in heads]
    for i, rs in enumerate(seqs):
        for h in heads:
            uv = jnp.concatenate([u[h][rs], vh[h][rs]], axis=0)
            bk = jnp.concatenate([bh[h][rs], kh[h][rs]], axis=0)
            g_last = g_incl[rs.stop - 1:rs.stop, lanes[h]]
            s_scr[i, h] = (s_old[i][h] + _mm(uv, bk, _TN)) * g_last
    o = jnp.concatenate(outs, axis=1)

    inv_n = 1.0 / RWKV_HEAD
    (o_sum,) = _head_sums([o], seg)
    cen = o - o_sum * inv_n
    (sq_sum,) = _head_sums([cen * cen], seg)
    y = cen * lax.rsqrt(sq_sum * inv_n + GN_EPS) * gng_ref[...] + gnb_ref[...] + bonus
    o_ref[...] = (y * gate).astype(o_ref.dtype)
    s_out_ref[...] = s_scr[...]


def _rwkv_group(zr, row0, batch, seq, nb, tlen, zp0, s0, p, seg, *, layer, depth, mixed, state,
                name):
    nc = seq // tlen
    rows = nb * tlen
    base = row0 // rows
    assert seq % tlen == 0 and batch % nb == 0 and row0 % rows == 0 and (nb == 1 or nc == 1)
    const = lambda bi, j: (0, 0)
    vec = lambda n: pl.BlockSpec((1, n), const)
    mat = lambda a: pl.BlockSpec(a.shape, const)
    st = (nb, RWKV_HEADS, RWKV_HEAD, RWKV_HEAD)
    carried = {0: mixed}
    if state is not None:
        carried[1] = state
    return _call_with_carried(
        functools.partial(_rwkv_body, nb=nb, tlen=tlen),
        grid=(batch // nb, nc),
        in_specs=[pl.BlockSpec((rows, RWKV_COLS), lambda bi, j: (base + bi * nc + j, 0)),
                  pl.BlockSpec((nb, 1, RWKV_COLS), lambda bi, j: (bi, 0, 0)),
                  pl.BlockSpec(st, lambda bi, j: (bi, 0, 0, 0)),
                  vec(RWKV_COLS), vec(RWKV_DIM), mat(p["w_up"]), vec(RWKV_DIM), mat(p["a_up"]),
                  mat(p["g_up"]), vec(RWKV_DIM), vec(RWKV_DIM), vec(RWKV_DIM), vec(RWKV_DIM),
                  vec(RWKV_DIM), mat(seg)],
        inputs=[zr, zp0, s0, p["mu"], p["w0"], p["w_up"], p["a0"], p["a_up"], p["g_up"],
                p["k_k"], p["k_a"], p["r_k"], p["gn_g"], p["gn_b"], seg],
        out_specs=[pl.BlockSpec((rows, RWKV_DIM), lambda bi, j: (base + bi * nc + j, 1)),
                   pl.BlockSpec((None,) + st, lambda bi, j: (layer, bi, 0, 0, 0))],
        out_shape=[jax.ShapeDtypeStruct(mixed.shape, BF16),
                   jax.ShapeDtypeStruct((depth, batch) + st[1:], F32)],
        carried=carried,
        scratch_shapes=[pltpu.VMEM(st, F32), pltpu.VMEM((nb, 1, RWKV_COLS), F32)],
        semantics=("parallel", "arbitrary"), name=name)


def _attn_body(q_ref, k_ref, v_ref, o_ref, *, bb, tq):
    scale = MEM_HEAD_DIM ** -0.5
    for i in range(bb):
        rows = slice(i * tq, (i + 1) * tq)
        for h in range(MEM_HEADS):
            sl = slice(h * MEM_HEAD_DIM, (h + 1) * MEM_HEAD_DIM)
            kh = k_ref[i, :, sl].astype(BF16)
            vh = v_ref[i, :, sl].astype(BF16)
            s = lax.dot_general(q_ref[rows, sl], kh, _NT, preferred_element_type=F32) * scale
            e = jnp.exp(s - jnp.max(s, axis=-1, keepdims=True))
            p = e / jnp.sum(e, axis=-1, keepdims=True)
            o_ref[rows, sl] = _dot(p.astype(BF16), vh).astype(o_ref.dtype)


def _attn_group(q, row0, batch, seq, tq, bb, mk, mv, *, out, name):
    nt = seq // tq
    rows = bb * tq
    base = row0 // rows
    assert seq % tq == 0 and batch % bb == 0 and row0 % rows == 0 and (bb == 1 or nt == 1)
    kv_spec = pl.BlockSpec((bb, N_MEM, D_MODEL), lambda bi, j: (bi, 0, 0))
    row_spec = pl.BlockSpec((rows, D_MODEL), lambda bi, j: (base + bi * nt + j, 0))
    return _call_with_carried(
        functools.partial(_attn_body, bb=bb, tq=tq),
        grid=(batch // bb, nt),
        in_specs=[row_spec, kv_spec, kv_spec], inputs=[q, mk, mv],
        out_specs=row_spec, out_shape=jax.ShapeDtypeStruct(q.shape, BF16),
        carried={} if out is None else {0: out},
        semantics=("parallel", "arbitrary"), name=name)


def kernel(x_prompt, x_sample, cache_mem_k, cache_mem_v, state_wkv, state_conv, state_shift,
           mem_prompt, w_in, mu_shift, conv_w, conv_b, conv_ln_g, conv_ln_b, rwkv_w0, rwkv_w_up,
           rwkv_a0, rwkv_a_up, rwkv_g_up, rwkv_k_k, rwkv_k_a, rwkv_r_k, rwkv_gn_g, rwkv_gn_b,
           w_out, mem_norm_g, w_q, w_k, w_v, w_o, w_ffn1, w_ffn2, norm_g):
    bp, tp, _ = x_prompt.shape
    bs, ts, _ = x_sample.shape
    depth = w_in.shape[0]
    n_p = bp * tp
    row = lambda a: a.reshape(1, -1)
    sample_nb = RWKV_ROWS // ts
    prompt_chunk = min(RWKV_ROWS, tp)

    x = jnp.concatenate([x_prompt.reshape(n_p, D_MODEL), x_sample.reshape(bs * ts, D_MODEL)], axis=0)
    mem2d = mem_prompt.reshape(bp * N_MEM, D_MODEL)
    lane_head = jnp.arange(V7X_LANES) // RWKV_HEAD
    seg = (lane_head[:, None] == lane_head[None, :]).astype(BF16)
    zero_conv = jnp.zeros((bp, CONV_BUF, CONV_CH), F32)
    zero_prev = jnp.zeros((bp, 1, RWKV_COLS), F32)
    zero_wkv = jnp.zeros((bp, RWKV_HEADS, RWKV_HEAD, RWKV_HEAD), F32)

    mk_p, mv_p, shift_p, shift_s = [], [], [], []
    wkv_p = conv_p = wkv_s = conv_s = None
    for l in range(depth):
        g = norm_g[l]
        w_in_b = w_in[l].astype(BF16)
        w_in_conv, w_in_rwkv = w_in_b[:, :2 * CONV_CH], w_in_b[:, 2 * CONV_CH:]
        rp = dict(mu=row(mu_shift[l]), w0=row(rwkv_w0[l]), w_up=rwkv_w_up[l].astype(BF16),
                  a0=row(rwkv_a0[l]), a_up=rwkv_a_up[l].astype(BF16),
                  g_up=rwkv_g_up[l].astype(BF16), k_k=row(rwkv_k_k[l]), k_a=row(rwkv_k_a[l]),
                  r_k=row(rwkv_r_k[l]), gn_g=row(rwkv_gn_g[l]), gn_b=row(rwkv_gn_b[l]))
        cw, cb = conv_w[l], row(conv_b[l])
        clg, clb = row(conv_ln_g[l]), row(conv_ln_b[l])

        last = jnp.concatenate([x[:n_p].reshape(bp, tp, D_MODEL)[:, -1],
                                x[n_p:].reshape(bs, ts, D_MODEL)[:, -1]], axis=0)
        shift = _rms_rows(last, g[0:1])
        shift_p.append(shift[:bp])
        shift_s.append(shift[bp:])

        zc, zr = _norm_matmul(x, g[0:1], [w_in_conv, w_in_rwkv], out_dtype=F32, name="in_proj")
        (zp_s,) = _norm_matmul(state_shift[l], g[0:1], [w_in_rwkv], out_dtype=F32,
                               normalize=False, name="prev_proj")

        mixed, conv_p = _conv_group(zc, 0, bp, tp, 1, min(CONV_TILE, tp), zero_conv, cw, cb, clg,
                                    clb, layer=l, depth=depth, mixed=None, state=conv_p,
                                    name="conv_prompt")
        mixed, conv_s = _conv_group(zc, n_p, bs, ts, sample_nb, ts, state_conv[l], cw, cb, clg,
                                    clb, layer=l, depth=depth, mixed=mixed, state=conv_s,
                                    name="conv_sample")
        mixed, wkv_p = _rwkv_group(zr, 0, bp, tp, 1, prompt_chunk, zero_prev, zero_wkv, rp, seg,
                                   layer=l, depth=depth, mixed=mixed, state=wkv_p,
                                   name="rwkv_prompt")
        mixed, wkv_s = _rwkv_group(zr, n_p, bs, ts, sample_nb, ts, zp_s[:, None, :], state_wkv[l],
                                   rp, seg, layer=l, depth=depth, mixed=mixed, state=wkv_s,
                                   name="rwkv_sample")
        x = _matmul_norm_res(mixed, w_out[l].astype(BF16), x, g[1:2], name="out_proj")

        (q,) = _norm_matmul(x, g[2:3], [w_q[l].astype(BF16)], out_dtype=BF16, name="q_proj")
        mk, mv = _norm_matmul(mem2d, row(mem_norm_g), [w_k[l].astype(BF16), w_v[l].astype(BF16)],
                              out_dtype=F32, name="mem_kv_proj")
        mk_p.append(mk.reshape(bp, N_MEM, MEM_HEADS, MEM_HEAD_DIM))
        mv_p.append(mv.reshape(bp, N_MEM, MEM_HEADS, MEM_HEAD_DIM))
        att = _attn_group(q, 0, bp, tp, min(ATTN_TILE, tp), 1, mk.reshape(bp, N_MEM, D_MODEL),
                          mv.reshape(bp, N_MEM, D_MODEL), out=None, name="attn_prompt")
        att = _attn_group(q, n_p, bs, ts, ts, ATTN_SAMPLE_BATCH,
                          cache_mem_k[l].reshape(bs, N_MEM, D_MODEL),
                          cache_mem_v[l].reshape(bs, N_MEM, D_MODEL), out=att, name="attn_sample")
        x = _matmul_norm_res(att, w_o[l].astype(BF16), x, g[3:4], name="attn_out_proj")

        (f,) = _norm_matmul(x, g[4:5], [w_ffn1[l].astype(BF16)], out_dtype=BF16, relu2=True,
                            name="ffn_up")
        x = _matmul_norm_res(f, w_ffn2[l].astype(BF16), x, g[5:6], name="ffn_down")

    y_prompt = x[:n_p].reshape(bp, tp, D_MODEL)
    y_sample = x[n_p:].reshape(bs, ts, D_MODEL)
    return (y_prompt, y_sample, jnp.stack(mk_p), jnp.stack(mv_p), wkv_p, conv_p,
            jnp.stack(shift_p), wkv_s, conv_s, jnp.stack(shift_s))
```

```python
import functools
import math

import jax
import jax.numpy as jnp
from jax import lax
from jax.experimental import pallas as pl
from jax.experimental.pallas import tpu as pltpu

F32 = jnp.float32
BF16 = jnp.bfloat16

D_MODEL = 1024
CONV_CH = 512
RWKV_DIM = 512
RWKV_HEAD = 64
RWKV_HEADS = 8
CONV_WIDTH = 31
CONV_BUF = CONV_WIDTH - 1
DECAY_RANK = 64
ICLR_RANK = 64
GATE_RANK = 128
RWKV_COLS = 3 * RWKV_DIM + DECAY_RANK + ICLR_RANK + GATE_RANK
IN_COLS = 2 * CONV_CH + RWKV_COLS
N_MEM = 256
MEM_HEADS = 4
MEM_HEAD_DIM = D_MODEL // MEM_HEADS
D_FF = 4 * D_MODEL
RMS_EPS = 1e-6
LN_EPS = 1e-5
GN_EPS = 64e-5

V7X_LANES = 128
V7X_SUBLANES = 8
V7X_VMEM_LIMIT_BYTES = 56 * 1024 * 1024

ROW_BLOCK = 512
COL_CHUNK = 1024
CONV_TILE = 256
RWKV_CHUNK = 64
RWKV_PROMPT_SEQS = 2
RWKV_SAMPLE_ROWS = 64
INV_BASE = 8
ATTN_TILE = 512
ATTN_SAMPLE_BATCH = 4

_NT = (((1,), (1,)), ((), ()))
_TN = (((0,), (0,)), ((), ()))


def _params(semantics):
    return pltpu.CompilerParams(dimension_semantics=semantics,
                                vmem_limit_bytes=V7X_VMEM_LIMIT_BYTES)


def _dot(a, b):
    return jnp.dot(a, b, preferred_element_type=F32)


def _mm(a, b, dims=None):
    if dims is None:
        dims = (((a.ndim - 1,), (0,)), ((), ()))
    return lax.dot_general(a.astype(BF16), b.astype(BF16), dims, preferred_element_type=F32)


def _split2(x):
    hi = x.astype(BF16)
    lo = (x - hi.astype(F32)).astype(BF16)
    return hi, lo


def _split3(x):
    hi = x.astype(BF16)
    r1 = x - hi.astype(F32)
    mid = r1.astype(BF16)
    lo = (r1 - mid.astype(F32)).astype(BF16)
    return hi, mid, lo


def _rms(x, g):
    ms = jnp.mean(x * x, axis=-1, keepdims=True)
    return x * lax.rsqrt(ms + RMS_EPS) * g


def _drop_carried(body, n_in, n_carried):
    def wrapped(*refs):
        return body(*refs[:n_in], *refs[n_in + n_carried:])
    return wrapped


def _call_with_carried(body, *, grid, in_specs, inputs, out_specs, out_shape, carried,
                       scratch_shapes=(), semantics, name):
    n_in = len(inputs)
    extra, aliases = [], {}
    for out_idx, buf in sorted(carried.items()):
        if buf is None:
            continue
        aliases[n_in + len(extra)] = out_idx
        extra.append(buf)
    return pl.pallas_call(
        _drop_carried(body, n_in, len(extra)),
        grid=grid,
        in_specs=list(in_specs) + [pl.BlockSpec(memory_space=pl.ANY)] * len(extra),
        out_specs=out_specs, out_shape=out_shape, scratch_shapes=list(scratch_shapes),
        input_output_aliases=aliases, compiler_params=_params(semantics), name=name,
    )(*inputs, *extra)


def _norm_matmul_body(x_ref, g_ref, w_ref, *o_refs, splits, normalize, relu2):
    x = x_ref[...]
    h = _rms(x, g_ref[...]) if normalize else x
    hb = h.astype(BF16)
    for (lo, hi), o_ref in zip(splits, o_refs):
        for c0 in range(lo, hi, COL_CHUNK):
            c1 = min(hi, c0 + COL_CHUNK)
            y = _dot(hb, w_ref[:, c0:c1])
            if relu2:
                y = jnp.square(jnp.maximum(y, 0.0))
            o_ref[:, c0 - lo:c1 - lo] = y.astype(o_ref.dtype)


def _norm_matmul(x, g, w, splits, *, out_dtype, normalize=True, relu2=False, name):
    m, k = x.shape
    bm = math.gcd(ROW_BLOCK, m)
    return pl.pallas_call(
        functools.partial(_norm_matmul_body, splits=tuple(splits), normalize=normalize,
                          relu2=relu2),
        grid=(m // bm,),
        in_specs=[pl.BlockSpec((bm, k), lambda i: (i, 0)),
                  pl.BlockSpec((1, k), lambda i: (0, 0)),
                  pl.BlockSpec(w.shape, lambda i: (0, 0))],
        out_specs=[pl.BlockSpec((bm, hi - lo), lambda i: (i, 0)) for lo, hi in splits],
        out_shape=[jax.ShapeDtypeStruct((m, hi - lo), out_dtype) for lo, hi in splits],
        compiler_params=_params(("parallel",)), name=name,
    )(x, g, w)


def _matmul_norm_res_body(*refs, n_lhs, first_blocks):
    y_refs = refs[:n_lhs]
    w_ref, x_ref, g_ref, o_ref = refs[n_lhs:]

    def emit(y_ref):
        m = _dot(y_ref[...], w_ref[...])
        o_ref[...] = x_ref[...] + _rms(m, g_ref[...])

    if n_lhs == 1:
        emit(y_refs[0])
    else:
        i = pl.program_id(0)
        pl.when(i < first_blocks)(lambda: emit(y_refs[0]))
        pl.when(i >= first_blocks)(lambda: emit(y_refs[1]))


def _matmul_norm_res(lhs, w, x, g, *, name):
    m, n = x.shape
    k = w.shape[0]
    bm = math.gcd(ROW_BLOCK, *[a.shape[0] for a in lhs])
    assert sum(a.shape[0] for a in lhs) == m
    first_blocks = lhs[0].shape[0] // bm
    if len(lhs) == 1:
        lhs_specs = [pl.BlockSpec((bm, k), lambda i: (i, 0))]
    else:
        lhs_specs = [pl.BlockSpec((bm, k), lambda i: (jnp.minimum(i, first_blocks - 1), 0)),
                     pl.BlockSpec((bm, k), lambda i: (jnp.maximum(i - first_blocks, 0), 0))]
    return pl.pallas_call(
        functools.partial(_matmul_norm_res_body, n_lhs=len(lhs), first_blocks=first_blocks),
        grid=(m // bm,),
        in_specs=lhs_specs + [pl.BlockSpec((k, n), lambda i: (0, 0)),
                              pl.BlockSpec((bm, n), lambda i: (i, 0)),
                              pl.BlockSpec((1, n), lambda i: (0, 0))],
        out_specs=pl.BlockSpec((bm, n), lambda i: (i, 0)),
        out_shape=jax.ShapeDtypeStruct((m, n), F32),
        compiler_params=_params(("parallel",)), name=name,
    )(*lhs, w, x, g)


def _rms_rows_body(x_ref, g_ref, o_ref):
    o_ref[...] = _rms(x_ref[...], g_ref[...])


def _rms_rows(x, g):
    m, k = x.shape
    return pl.pallas_call(
        _rms_rows_body, grid=(1,),
        in_specs=[pl.BlockSpec((m, k), lambda i: (0, 0)),
                  pl.BlockSpec((1, k), lambda i: (0, 0))],
        out_specs=pl.BlockSpec((m, k), lambda i: (0, 0)),
        out_shape=jax.ShapeDtypeStruct((m, k), F32),
        compiler_params=_params(("arbitrary",)), name="rms_rows",
    )(x, g)


def _mem_kv_body(x_ref, g_ref, wk_ref, wv_ref, k2_ref, v2_ref, k5_ref, v5_ref, *, nb):
    hb = _rms(x_ref[...], g_ref[...]).astype(BF16)
    for w_ref, o2_ref, o5_ref in ((wk_ref, k2_ref, k5_ref), (wv_ref, v2_ref, v5_ref)):
        y = _dot(hb, w_ref[...])
        o2_ref[...] = y.astype(o2_ref.dtype)
        for i in range(nb):
            for h in range(MEM_HEADS):
                o5_ref[i, :, h, :] = y[i * N_MEM:(i + 1) * N_MEM,
                                       h * MEM_HEAD_DIM:(h + 1) * MEM_HEAD_DIM]


def _mem_kv_proj(mem2d, g, wk, wv, *, layer, depth, k5, v5):
    m, k = mem2d.shape
    nb = ROW_BLOCK // N_MEM
    batch = m // N_MEM
    assert batch % nb == 0
    row_spec = pl.BlockSpec((nb * N_MEM, D_MODEL), lambda i: (i, 0))
    w_spec = pl.BlockSpec((k, D_MODEL), lambda i: (0, 0))
    out5 = pl.BlockSpec((None, nb, N_MEM, MEM_HEADS, MEM_HEAD_DIM), lambda i: (layer, i, 0, 0, 0))
    shape5 = jax.ShapeDtypeStruct((depth, batch, N_MEM, MEM_HEADS, MEM_HEAD_DIM), F32)
    return _call_with_carried(
        functools.partial(_mem_kv_body, nb=nb),
        grid=(batch // nb,),
        in_specs=[row_spec, pl.BlockSpec((1, k), lambda i: (0, 0)), w_spec, w_spec],
        inputs=[mem2d, g, wk, wv],
        out_specs=[row_spec, row_spec, out5, out5],
        out_shape=[jax.ShapeDtypeStruct((m, D_MODEL), BF16), jax.ShapeDtypeStruct((m, D_MODEL), BF16),
                   shape5, shape5],
        carried={2: k5, 3: v5}, semantics=("parallel",), name="mem_kv_proj")


def _conv_body(z_ref, buf_ref, w_ref, b_ref, lg_ref, lb_ref, y_ref, nc_ref, ext_ref, sh_ref,
               *, nb, tt):
    cur = CONV_BUF + (-CONV_BUF) % V7X_SUBLANES
    h0 = cur - CONV_BUF
    for i in range(nb):
        rows = slice(i * tt, (i + 1) * tt)
        if nb == 1:
            @pl.when(pl.program_id(1) == 0)
            def _():
                ext_ref[h0:cur, :] = buf_ref[0]
        else:
            ext_ref[h0:cur, :] = buf_ref[i]
        z = z_ref[rows, :]
        u = z[:, :CONV_CH] * jax.nn.sigmoid(z[:, CONV_CH:])
        ext_ref[cur:cur + tt, :] = u
        acc = None
        for r in range(V7X_SUBLANES):
            taps = range(r, CONV_WIDTH, V7X_SUBLANES)
            span = tt + taps[-1] - r
            sh_ref[r, 0:span, :] = ext_ref[h0 + r:h0 + r + span, :]
            for k in taps:
                term = sh_ref[r, k - r:k - r + tt, :] * w_ref[k:k + 1, :]
                acc = term if acc is None else acc + term
        y = acc + b_ref[...]
        mu = jnp.mean(y, axis=-1, keepdims=True)
        var = jnp.mean(jnp.square(y - mu), axis=-1, keepdims=True)
        yn = (y - mu) * lax.rsqrt(var + LN_EPS) * lg_ref[...] + lb_ref[...]
        y_ref[rows, :] = (yn * jax.nn.sigmoid(yn)).astype(y_ref.dtype)
        halo = ext_ref[tt + h0:tt + cur, :]
        if nb == 1:
            ext_ref[h0:cur, :] = halo
        nc_ref[i] = halo


def _conv_group(zc, row0, batch, seq, nb, tt, conv_buf, w, b, lg, lb, *, layer, depth, state,
                seq_major, name):
    nt = seq // tt
    rows = nb * tt
    base = row0 // rows
    assert seq % tt == 0 and batch % nb == 0 and row0 % rows == 0 and (nb == 1 or nt == 1)
    const = lambda bi, j: (0, 0)
    width = CONV_CH + RWKV_DIM
    if seq_major:
        assert nb == 1
        y_spec = pl.BlockSpec((None, tt, CONV_CH), lambda bi, j: (bi, j, 0))
        y_shape = jax.ShapeDtypeStruct((batch, seq, width), BF16)
    else:
        y_spec = pl.BlockSpec((rows, CONV_CH), lambda bi, j: (bi, 0))
        y_shape = jax.ShapeDtypeStruct((batch * seq, width), BF16)
    halo_rows = CONV_BUF + (-CONV_BUF) % V7X_SUBLANES
    return _call_with_carried(
        functools.partial(_conv_body, nb=nb, tt=tt),
        grid=(batch // nb, nt),
        in_specs=[pl.BlockSpec((rows, 2 * CONV_CH), lambda bi, j: (base + bi * nt + j, 0)),
                  pl.BlockSpec((nb, CONV_BUF, CONV_CH), lambda bi, j: (bi, 0, 0)),
                  pl.BlockSpec((CONV_WIDTH, CONV_CH), const),
                  pl.BlockSpec((1, CONV_CH), const), pl.BlockSpec((1, CONV_CH), const),
                  pl.BlockSpec((1, CONV_CH), const)],
        inputs=[zc, conv_buf, w, b, lg, lb],
        out_specs=[y_spec,
                   pl.BlockSpec((None, nb, CONV_BUF, CONV_CH), lambda bi, j: (layer, bi, 0, 0))],
        out_shape=[y_shape, jax.ShapeDtypeStruct((depth, batch, CONV_BUF, CONV_CH), F32)],
        carried={1: state},
        scratch_shapes=[pltpu.VMEM((halo_rows + tt, CONV_CH), F32),
                        pltpu.VMEM((V7X_SUBLANES, tt + CONV_BUF, CONV_CH), F32)],
        semantics=("parallel", "arbitrary"), name=name)


def _head_sums(xs, seg):
    rows = xs[0].shape[0]
    groups = RWKV_DIM // V7X_LANES
    stacked = jnp.concatenate(
        [x[:, g * V7X_LANES:(g + 1) * V7X_LANES] for x in xs for g in range(groups)], axis=0)
    hi, lo = _split2(stacked)
    y = _dot(hi, seg) + _dot(lo, seg)
    outs = []
    for i in range(len(xs)):
        parts = [y[(i * groups + g) * rows:(i * groups + g + 1) * rows] for g in range(groups)]
        outs.append(jnp.concatenate(parts, axis=1))
    return outs


def _rwkv_body(*refs, nb, tlen, n_zr):
    zr_refs = refs[:n_zr]
    (zp0_ref, s0_ref, mu_ref, w0_ref, wup_ref, a0_ref, aup_ref, gup_ref, kk_ref, ka_ref, rk_ref,
     gng_ref, gnb_ref, seg_ref, o_ref, s_out_ref, s_scr, carry_scr) = refs[n_zr:]
    rows = nb * tlen
    shift = int(math.log2(tlen))
    assert 1 << shift == tlen

    @pl.when(pl.program_id(1) == 0)
    def _():
        s_scr[...] = s0_ref[...]
        carry_scr[...] = zp0_ref[...]

    if n_zr == 1:
        zr = zr_refs[0][...]
    else:
        zr = jnp.concatenate([r[...] for r in zr_refs], axis=0)
    seqs = [slice(i * tlen, (i + 1) * tlen) for i in range(nb)]
    t_idx = lax.broadcasted_iota(jnp.int32, (rows, 1), 0) & (tlen - 1)
    if nb == 1:
        carry_rows = carry_scr[0]
    else:
        carry_rows = jnp.concatenate(
            [jnp.broadcast_to(carry_scr[i], (tlen, RWKV_COLS)) for i in range(nb)], axis=0)
    prev = jnp.where(t_idx == 0, carry_rows, pltpu.roll(zr, 1, axis=0))
    for i, rs in enumerate(seqs):
        carry_scr[i] = zr[rs.stop - 1:rs.stop, :]
    zs = zr + (prev - zr) * mu_ref[...]

    d = RWKV_DIM
    r = zs[:, 0:d]
    k = zs[:, d:2 * d]
    v = zs[:, 2 * d:3 * d]
    dw = zs[:, 3 * d:3 * d + DECAY_RANK]
    da = zs[:, 3 * d + DECAY_RANK:3 * d + DECAY_RANK + ICLR_RANK]
    dg = zs[:, 3 * d + DECAY_RANK + ICLR_RANK:]

    lw = w0_ref[...] + _mm(jnp.tanh(dw), wup_ref[...])
    w_raw = jnp.minimum(lw, 0.0) - jnp.log(1.0 + jnp.exp(-jnp.abs(lw))) - 0.5
    logw = -jnp.exp(w_raw)
    a = jax.nn.sigmoid(a0_ref[...] + _mm(da, aup_ref[...]))
    gate = _mm(jax.nn.sigmoid(dg), gup_ref[...])

    seg = seg_ref[...]
    kk = k * kk_ref[...]
    kmod = k * (1.0 + (a - 1.0) * ka_ref[...])
    kk_sq, bonus_dot = _head_sums([kk * kk, r * kmod * rk_ref[...]], seg)
    kk = kk / jnp.maximum(jnp.sqrt(kk_sq), 1e-12)
    bonus = bonus_dot * v

    ri = lax.broadcasted_iota(jnp.int32, (rows, rows), 0)
    ci = lax.broadcasted_iota(jnp.int32, (rows, rows), 1)
    same_seq = (ri >> shift) == (ci >> shift)
    incl = same_seq & (ri >= ci)
    strict = same_seq & (ri > ci)
    tri = incl.astype(BF16)
    w_hi, w_mid, w_lo = _split3(logw)
    cl = _dot(tri, w_hi) + _dot(tri, w_mid) + _dot(tri, w_lo)
    g_incl = jnp.exp(cl)
    g_inv = jnp.exp(-cl)
    a_t = -kk * jnp.exp(cl - logw)
    b_t = kk * a * g_inv
    k_t = kmod * g_inv
    r_t = r * g_incl

    eye = (ri == ci).astype(F32)
    base_shift = min(shift, int(math.log2(INV_BASE)))
    base_blk = (ri >> base_shift) == (ci >> base_shift)
    level_blks = [((ri >> (lv + 1)) == (ci >> (lv + 1))) & ((ri >> lv) != (ci >> lv))
                  for lv in range(base_shift, shift)]
    heads = range(RWKV_HEADS)
    lanes = [slice(h * RWKV_HEAD, (h + 1) * RWKV_HEAD) for h in heads]
    ah = [a_t[:, sl] for sl in lanes]
    bh = [b_t[:, sl] for sl in lanes]
    kh = [k_t[:, sl] for sl in lanes]
    rh = [r_t[:, sl] for sl in lanes]
    vh = [v[:, sl] for sl in lanes]
    ar = [jnp.concatenate([ah[h], rh[h]], axis=0) for h in heads]
    gb = [_mm(ar[h], bh[h], _NT) for h in heads]
    gk = [_mm(ar[h], kh[h], _NT) for h in heads]
    a_ab = [jnp.where(strict, gb[h][:rows], 0.0) for h in heads]
    a_rb = [jnp.where(incl, gb[h][rows:], 0.0) for h in heads]
    a_ak = [jnp.where(strict, gk[h][:rows], 0.0) for h in heads]
    a_rk = [jnp.where(incl, gk[h][rows:], 0.0) for h in heads]
    s_old = [[s_scr[i, h] for h in heads] for i in range(nb)]
    s_terms = [[_mm(jnp.concatenate([ah[h][rs], rh[h][rs]], axis=0), s_old[i][h], _NT)
                for i, rs in enumerate(seqs)] for h in heads]
    a_s0 = [jnp.concatenate([x[:tlen] for x in s_terms[h]], axis=0) for h in heads]
    r_s0 = [jnp.concatenate([x[tlen:] for x in s_terms[h]], axis=0) for h in heads]
    av = [_mm(jnp.concatenate([a_ak[h], a_rk[h]], axis=0), vh[h]) for h in heads]
    rhs = [a_s0[h] + av[h][:rows] for h in heads]
    npow = [jnp.where(base_blk, a_ab[h], 0.0) for h in heads]
    inv = [eye + npow[h] for h in heads]
    for _ in range(max(0, base_shift - 1)):
        npow = [_mm(npow[h], npow[h]) for h in heads]
        inv = [inv[h] + _mm(inv[h], npow[h]) for h in heads]
    for off_blk in level_blks:
        cross = [_mm(jnp.where(off_blk, a_ab[h], 0.0), inv[h]) for h in heads]
        inv = [inv[h] + _mm(inv[h], cross[h]) for h in heads]
    u = [_mm(inv[h], rhs[h]) for h in heads]
    outs = [r_s0[h] + _mm(a_rb[h], u[h]) + av[h][rows:] for h in heads]
    for i, rs in enumerate(seqs):
        for h in heads:
            uv = jnp.concatenate([u[h][rs], vh[h][rs]], axis=0)
            bk = jnp.concatenate([bh[h][rs], kh[h][rs]], axis=0)
            g_last = g_incl[rs.stop - 1:rs.stop, lanes[h]]
            s_scr[i, h] = (s_old[i][h] + _mm(uv, bk, _TN)) * g_last
    o = jnp.concatenate(outs, axis=1)

    inv_n = 1.0 / RWKV_HEAD
    (o_sum,) = _head_sums([o], seg)
    cen = o - o_sum * inv_n
    (sq_sum,) = _head_sums([cen * cen], seg)
    y = cen * lax.rsqrt(sq_sum * inv_n + GN_EPS) * gng_ref[...] + gnb_ref[...] + bonus
    y = (y * gate).astype(o_ref.dtype)
    if len(o_ref.shape) == 3:
        for i, rs in enumerate(seqs):
            o_ref[i] = y[rs]
    else:
        o_ref[...] = y
    s_out_ref[...] = s_scr[...]


def _rwkv_group(zr, row0, batch, seq, nb, tlen, zp0, s0, p, seg, *, layer, depth, mixed, state,
                name):
    nc = seq // tlen
    assert seq % tlen == 0 and batch % nb == 0 and row0 % tlen == 0
    const = lambda bi, j: (0, 0)
    vec = lambda n: pl.BlockSpec((1, n), const)
    mat = lambda a: pl.BlockSpec(a.shape, const)
    st = (nb, RWKV_HEADS, RWKV_HEAD, RWKV_HEAD)
    base = row0 // tlen
    if nc > 1:
        zr_specs = [pl.BlockSpec((tlen, RWKV_COLS),
                                 lambda bi, j, i=i: (base + (bi * nb + i) * nc + j, 0))
                    for i in range(nb)]
    else:
        assert row0 % (nb * tlen) == 0
        zr_specs = [pl.BlockSpec((nb * tlen, RWKV_COLS),
                                 lambda bi, j: (row0 // (nb * tlen) + bi, 0))]
    if mixed.ndim == 3:
        o_spec = pl.BlockSpec((nb, tlen, RWKV_DIM), lambda bi, j: (bi, j, 1))
    else:
        assert nc == 1
        o_spec = pl.BlockSpec((nb * tlen, RWKV_DIM), lambda bi, j: (bi, 1))
    n_zr = len(zr_specs)
    return _call_with_carried(
        functools.partial(_rwkv_body, nb=nb, tlen=tlen, n_zr=n_zr),
        grid=(batch // nb, nc),
        in_specs=zr_specs + [
            pl.BlockSpec((nb, 1, RWKV_COLS), lambda bi, j: (bi, 0, 0)),
            pl.BlockSpec(st, lambda bi, j: (bi, 0, 0, 0)),
            vec(RWKV_COLS), vec(RWKV_DIM), mat(p["w_up"]), vec(RWKV_DIM), mat(p["a_up"]),
            mat(p["g_up"]), vec(RWKV_DIM), vec(RWKV_DIM), vec(RWKV_DIM), vec(RWKV_DIM),
            vec(RWKV_DIM), mat(seg)],
        inputs=[zr] * n_zr + [zp0, s0, p["mu"], p["w0"], p["w_up"], p["a0"], p["a_up"], p["g_up"],
                              p["k_k"], p["k_a"], p["r_k"], p["gn_g"], p["gn_b"], seg],
        out_specs=[o_spec, pl.BlockSpec((None,) + st, lambda bi, j: (layer, bi, 0, 0, 0))],
        out_shape=[jax.ShapeDtypeStruct(mixed.shape, BF16),
                   jax.ShapeDtypeStruct((depth, batch) + st[1:], F32)],
        carried={0: mixed, 1: state},
        scratch_shapes=[pltpu.VMEM(st, F32), pltpu.VMEM((nb, 1, RWKV_COLS), F32)],
        semantics=("parallel", "arbitrary"), name=name)


def _attn_body(q_ref, k_ref, v_ref, o_ref, *dense_refs, bb, tq):
    scale = MEM_HEAD_DIM ** -0.5
    pairs = [(i, h) for i in range(bb) for h in range(MEM_HEADS)]
    rows = lambda i: slice(i * tq, (i + 1) * tq)
    cols = lambda h: slice(h * MEM_HEAD_DIM, (h + 1) * MEM_HEAD_DIM)

    if dense_refs:
        for src, dst in zip((k_ref, v_ref), dense_refs):
            for i, h in pairs:
                dst[i, :, cols(h)] = src[i, :, h, :]
        k_ref, v_ref = dense_refs

    def head(ref, i, h):
        return ref[i, :, cols(h)].astype(BF16)

    s = [lax.dot_general(q_ref[rows(i), cols(h)], head(k_ref, i, h), _NT,
                         preferred_element_type=F32) * scale for i, h in pairs]
    e = [jnp.exp(x - jnp.max(x, axis=-1, keepdims=True)) for x in s]
    p = [x / jnp.sum(x, axis=-1, keepdims=True) for x in e]
    o = [_dot(p[n].astype(BF16), head(v_ref, i, h)) for n, (i, h) in enumerate(pairs)]
    for n, (i, h) in enumerate(pairs):
        o_ref[rows(i), cols(h)] = o[n].astype(o_ref.dtype)


def _attn_group(q, row0, batch, seq, tq, bb, mk, mv, *, layer, name):
    nt = seq // tq
    rows = bb * tq
    base = row0 // rows
    assert seq % tq == 0 and batch % bb == 0 and row0 % rows == 0 and (bb == 1 or nt == 1)
    if mk.ndim == 5:
        kv_spec = pl.BlockSpec((None, bb, N_MEM, MEM_HEADS, MEM_HEAD_DIM),
                               lambda bi, j: (layer, bi, 0, 0, 0))
        scratch = [pltpu.VMEM((bb, N_MEM, D_MODEL), F32)] * 2
    else:
        kv_spec = pl.BlockSpec((bb, N_MEM, D_MODEL), lambda bi, j: (bi, 0, 0))
        scratch = []
    return pl.pallas_call(
        functools.partial(_attn_body, bb=bb, tq=tq),
        grid=(batch // bb, nt),
        in_specs=[pl.BlockSpec((rows, D_MODEL), lambda bi, j: (base + bi * nt + j, 0)),
                  kv_spec, kv_spec],
        out_specs=pl.BlockSpec((rows, D_MODEL), lambda bi, j: (bi * nt + j, 0)),
        out_shape=jax.ShapeDtypeStruct((batch * seq, D_MODEL), BF16),
        scratch_shapes=scratch,
        compiler_params=_params(("parallel", "arbitrary")), name=name,
    )(q, mk, mv)


def kernel(x_prompt, x_sample, cache_mem_k, cache_mem_v, state_wkv, state_conv, state_shift,
           mem_prompt, w_in, mu_shift, conv_w, conv_b, conv_ln_g, conv_ln_b, rwkv_w0, rwkv_w_up,
           rwkv_a0, rwkv_a_up, rwkv_g_up, rwkv_k_k, rwkv_k_a, rwkv_r_k, rwkv_gn_g, rwkv_gn_b,
           w_out, mem_norm_g, w_q, w_k, w_v, w_o, w_ffn1, w_ffn2, norm_g):
    bp, tp, _ = x_prompt.shape
    bs, ts, _ = x_sample.shape
    depth = w_in.shape[0]
    n_p, n_s = bp * tp, bs * ts
    row = lambda a: a.reshape(1, -1)
    sample_nb = RWKV_SAMPLE_ROWS // ts
    prompt_chunk = min(RWKV_CHUNK, tp)
    prompt_nb = min(RWKV_PROMPT_SEQS, bp)

    x = jnp.concatenate([x_prompt.reshape(n_p, D_MODEL), x_sample.reshape(n_s, D_MODEL)], axis=0)
    mem2d = mem_prompt.reshape(bp * N_MEM, D_MODEL)
    lane_head = jnp.arange(V7X_LANES) // RWKV_HEAD
    seg = (lane_head[:, None] == lane_head[None, :]).astype(BF16)
    zero_conv = jnp.zeros((bp, CONV_BUF, CONV_CH), F32)
    zero_prev = jnp.zeros((bp, 1, RWKV_COLS), F32)
    zero_wkv = jnp.zeros((bp, RWKV_HEADS, RWKV_HEAD, RWKV_HEAD), F32)

    shift_p, shift_s = [], []
    mk5 = mv5 = wkv_p = conv_p = wkv_s = conv_s = None
    for l in range(depth):
        g = norm_g[l]
        w_in_b = w_in[l].astype(BF16)
        rp = dict(mu=row(mu_shift[l]), w0=row(rwkv_w0[l]), w_up=rwkv_w_up[l].astype(BF16),
                  a0=row(rwkv_a0[l]), a_up=rwkv_a_up[l].astype(BF16),
                  g_up=rwkv_g_up[l].astype(BF16), k_k=row(rwkv_k_k[l]), k_a=row(rwkv_k_a[l]),
                  r_k=row(rwkv_r_k[l]), gn_g=row(rwkv_gn_g[l]), gn_b=row(rwkv_gn_b[l]))
        cw, cb = conv_w[l], row(conv_b[l])
        clg, clb = row(conv_ln_g[l]), row(conv_ln_b[l])

        last = jnp.concatenate([lax.slice(x, (tp - 1, 0), (n_p, D_MODEL), (tp, 1)),
                                lax.slice(x, (n_p + ts - 1, 0), (n_p + n_s, D_MODEL), (ts, 1))],
                               axis=0)
        shift = _rms_rows(last, g[0:1])
        shift_p.append(shift[:bp])
        shift_s.append(shift[bp:])

        zc, zr = _norm_matmul(x, g[0:1], w_in_b, [(0, 2 * CONV_CH), (2 * CONV_CH, IN_COLS)],
                              out_dtype=F32, name="in_proj")
        (zp_s,) = _norm_matmul(state_shift[l], g[0:1], w_in_b, [(2 * CONV_CH, IN_COLS)],
                               out_dtype=F32, normalize=False, name="prev_proj")

        mixed_p, conv_p = _conv_group(zc, 0, bp, tp, 1, min(CONV_TILE, tp), zero_conv, cw, cb,
                                      clg, clb, layer=l, depth=depth, state=conv_p,
                                      seq_major=True, name="conv_prompt")
        mixed_s, conv_s = _conv_group(zc, n_p, bs, ts, sample_nb, ts, state_conv[l], cw, cb, clg,
                                      clb, layer=l, depth=depth, state=conv_s, seq_major=False,
                                      name="conv_sample")
        mixed_p, wkv_p = _rwkv_group(zr, 0, bp, tp, prompt_nb, prompt_chunk, zero_prev, zero_wkv,
                                     rp, seg, layer=l, depth=depth, mixed=mixed_p, state=wkv_p,
                                     name="rwkv_prompt")
        mixed_s, wkv_s = _rwkv_group(zr, n_p, bs, ts, sample_nb, ts, zp_s[:, None, :], state_wkv[l],
                                     rp, seg, layer=l, depth=depth, mixed=mixed_s, state=wkv_s,
                                     name="rwkv_sample")
        x = _matmul_norm_res([mixed_p.reshape(n_p, D_MODEL), mixed_s], w_out[l].astype(BF16), x,
                             g[1:2], name="out_proj")

        (q,) = _norm_matmul(x, g[2:3], w_q[l].astype(BF16), [(0, D_MODEL)], out_dtype=BF16,
                            name="q_proj")
        mk2, mv2, mk5, mv5 = _mem_kv_proj(mem2d, row(mem_norm_g), w_k[l].astype(BF16),
                                          w_v[l].astype(BF16), layer=l, depth=depth, k5=mk5, v5=mv5)
        att_p = _attn_group(q, 0, bp, tp, min(ATTN_TILE, tp), 1, mk2.reshape(bp, N_MEM, D_MODEL),
                            mv2.reshape(bp, N_MEM, D_MODEL), layer=l, name="attn_prompt")
        att_s = _attn_group(q, n_p, bs, ts, ts, ATTN_SAMPLE_BATCH, cache_mem_k, cache_mem_v,
                            layer=l, name="attn_sample")
        x = _matmul_norm_res([att_p, att_s], w_o[l].astype(BF16), x, g[3:4], name="attn_out_proj")

        (f,) = _norm_matmul(x, g[4:5], w_ffn1[l].astype(BF16), [(0, D_FF)], out_dtype=BF16,
                            relu2=True, name="ffn_up")
        x = _matmul_norm_res([f], w_ffn2[l].astype(BF16), x, g[5:6], name="ffn_down")

    y_prompt = x[:n_p].reshape(bp, tp, D_MODEL)
    y_sample = x[n_p:].reshape(bs, ts, D_MODEL)
    return (y_prompt, y_sample, mk5, mv5, wkv_p, conv_p, jnp.stack(shift_p), wkv_s, conv_s,
            jnp.stack(shift_s))
```

```python
import functools
import math

import jax
import jax.numpy as jnp
from jax import lax
from jax.experimental import pallas as pl
from jax.experimental.pallas import tpu as pltpu

F32 = jnp.float32
BF16 = jnp.bfloat16

D_MODEL = 1024
CONV_CH = 512
RWKV_DIM = 512
RWKV_HEAD = 64
RWKV_HEADS = 8
CONV_WIDTH = 31
CONV_BUF = CONV_WIDTH - 1
DECAY_RANK = 64
ICLR_RANK = 64
GATE_RANK = 128
RWKV_COLS = 3 * RWKV_DIM + DECAY_RANK + ICLR_RANK + GATE_RANK
IN_COLS = 2 * CONV_CH + RWKV_COLS
N_MEM = 256
MEM_HEADS = 4
MEM_HEAD_DIM = D_MODEL // MEM_HEADS
D_FF = 4 * D_MODEL
RMS_EPS = 1e-6
LN_EPS = 1e-5
GN_EPS = 64e-5

V7X_LANES = 128
V7X_SUBLANES = 8
V7X_VMEM_LIMIT_BYTES = 56 * 1024 * 1024

ROW_BLOCK = 512
COL_CHUNK = 1024
CONV_TILE = 256
RWKV_GROUP_ROWS = 64
RWKV_PROMPT_SEQS = 4
PACK_HEADS = 4
SAMPLE_STEP_ROWS = 64
INV_BASE = 8
ATTN_TILE = 512
ATTN_SAMPLE_BATCH = 4

_NT = (((1,), (1,)), ((), ()))
_TN = (((0,), (0,)), ((), ()))


def _params(semantics):
    return pltpu.CompilerParams(dimension_semantics=semantics,
                                vmem_limit_bytes=V7X_VMEM_LIMIT_BYTES)


def _dot(a, b):
    return jnp.dot(a, b, preferred_element_type=F32)


def _mm(a, b, dims=None):
    if dims is None:
        dims = (((a.ndim - 1,), (0,)), ((), ()))
    return lax.dot_general(a.astype(BF16), b.astype(BF16), dims, preferred_element_type=F32)


def _split2(x):
    hi = x.astype(BF16)
    lo = (x - hi.astype(F32)).astype(BF16)
    return hi, lo


def _split3(x):
    hi = x.astype(BF16)
    r1 = x - hi.astype(F32)
    mid = r1.astype(BF16)
    lo = (r1 - mid.astype(F32)).astype(BF16)
    return hi, mid, lo


def _rms(x, g):
    ms = jnp.mean(x * x, axis=-1, keepdims=True)
    return x * lax.rsqrt(ms + RMS_EPS) * g


def _drop_carried(body, n_in, n_carried):
    def wrapped(*refs):
        return body(*refs[:n_in], *refs[n_in + n_carried:])
    return wrapped


def _call_with_carried(body, *, grid, in_specs, inputs, out_specs, out_shape, carried,
                       scratch_shapes=(), semantics, name):
    n_in = len(inputs)
    extra, aliases = [], {}
    for out_idx, buf in sorted(carried.items()):
        if buf is None:
            continue
        aliases[n_in + len(extra)] = out_idx
        extra.append(buf)
    return pl.pallas_call(
        _drop_carried(body, n_in, len(extra)),
        grid=grid,
        in_specs=list(in_specs) + [pl.BlockSpec(memory_space=pl.ANY)] * len(extra),
        out_specs=out_specs, out_shape=out_shape, scratch_shapes=list(scratch_shapes),
        input_output_aliases=aliases, compiler_params=_params(semantics), name=name,
    )(*inputs, *extra)


def _resident(shape):
    return pl.BlockSpec(shape, lambda i: (0,) * len(shape), pipeline_mode=pl.Buffered(1))


def _project(hb, w_ref, o_refs, splits, relu2):
    for (lo, hi), o_ref in zip(splits, o_refs):
        for c0 in range(lo, hi, COL_CHUNK):
            c1 = min(hi, c0 + COL_CHUNK)
            y = _dot(hb, w_ref[:, c0:c1])
            if relu2:
                y = jnp.square(jnp.maximum(y, 0.0))
            o_ref[:, c0 - lo:c1 - lo] = y.astype(o_ref.dtype)


def _norm_matmul_body(x_ref, g_ref, w_ref, *o_refs, splits, normalize, relu2):
    x = x_ref[...]
    h = _rms(x, g_ref[...]) if normalize else x
    _project(h.astype(BF16), w_ref, o_refs, splits, relu2)


def _norm_matmul(x, g, w, splits, *, out_dtype, normalize=True, relu2=False, name):
    m, k = x.shape
    bm = math.gcd(ROW_BLOCK, m)
    return pl.pallas_call(
        functools.partial(_norm_matmul_body, splits=tuple(splits), normalize=normalize,
                          relu2=relu2),
        grid=(m // bm,),
        in_specs=[pl.BlockSpec((bm, k), lambda i: (i, 0)), _resident((1, k)), _resident(w.shape)],
        out_specs=[pl.BlockSpec((bm, hi - lo), lambda i: (i, 0)) for lo, hi in splits],
        out_shape=[jax.ShapeDtypeStruct((m, hi - lo), out_dtype) for lo, hi in splits],
        compiler_params=_params(("parallel",)), name=name,
    )(x, g, w)


def _res_proj_body(*refs, n_lhs, first_blocks, splits, relu2):
    y_refs = refs[:n_lhs]
    w_a_ref, x_ref, g_a_ref = refs[n_lhs:n_lhs + 3]
    rest = refs[n_lhs + 3:]
    if splits:
        g_b_ref, w_b_ref = rest[:2]
        rest = rest[2:]
    o_ref, p_refs = rest[0], rest[1:]

    def emit(y_ref):
        x_new = x_ref[...] + _rms(_dot(y_ref[...], w_a_ref[...]), g_a_ref[...])
        o_ref[...] = x_new
        if splits:
            _project(_rms(x_new, g_b_ref[...]).astype(BF16), w_b_ref, p_refs, splits, relu2)

    if n_lhs == 1:
        emit(y_refs[0])
    else:
        i = pl.program_id(0)
        pl.when(i < first_blocks)(lambda: emit(y_refs[0]))
        pl.when(i >= first_blocks)(lambda: emit(y_refs[1]))


def _res_proj(lhs, w_a, x, g_a, *, g_b=None, w_b=None, splits=(), relu2=False, out_dtype=None,
              row0=0, n_rows=None, name):
    m, n = x.shape
    k = w_a.shape[0]
    n_rows = m if n_rows is None else n_rows
    bm = math.gcd(ROW_BLOCK, n_rows, row0, *[a.shape[0] for a in lhs])
    if len(lhs) == 1:
        lhs_specs = [pl.BlockSpec((bm, k), lambda i: (row0 // bm + i, 0))]
        first_blocks = None
    else:
        assert row0 == 0 and n_rows == m and sum(a.shape[0] for a in lhs) == m
        first_blocks = lhs[0].shape[0] // bm
        lhs_specs = [pl.BlockSpec((bm, k), lambda i: (jnp.minimum(i, first_blocks - 1), 0)),
                     pl.BlockSpec((bm, k), lambda i: (jnp.maximum(i - first_blocks, 0), 0))]
    splits = tuple(splits)
    row_out = lambda width: pl.BlockSpec((bm, width), lambda i: (i, 0))
    proj_in = [_resident((1, n)), _resident(w_b.shape)] if splits else []
    proj_args = [g_b, w_b] if splits else []
    return pl.pallas_call(
        functools.partial(_res_proj_body, n_lhs=len(lhs), first_blocks=first_blocks,
                          splits=splits, relu2=relu2),
        grid=(n_rows // bm,),
        in_specs=lhs_specs + [_resident((k, n)),
                              pl.BlockSpec((bm, n), lambda i: (row0 // bm + i, 0)),
                              _resident((1, n))] + proj_in,
        out_specs=[row_out(n)] + [row_out(hi - lo) for lo, hi in splits],
        out_shape=[jax.ShapeDtypeStruct((n_rows, n), F32)]
        + [jax.ShapeDtypeStruct((n_rows, hi - lo), out_dtype) for lo, hi in splits],
        compiler_params=_params(("parallel",)), name=name,
    )(*lhs, w_a, x, g_a, *proj_args)


def _rms_rows_body(x_ref, g_ref, o_ref):
    o_ref[...] = _rms(x_ref[...], g_ref[...])


def _rms_rows(x, g):
    m, k = x.shape
    return pl.pallas_call(
        _rms_rows_body, grid=(1,),
        in_specs=[pl.BlockSpec((m, k), lambda i: (0, 0)),
                  pl.BlockSpec((1, k), lambda i: (0, 0))],
        out_specs=pl.BlockSpec((m, k), lambda i: (0, 0)),
        out_shape=jax.ShapeDtypeStruct((m, k), F32),
        compiler_params=_params(("arbitrary",)), name="rms_rows",
    )(x, g)


def _mem_kv_body(x_ref, g_ref, wk_ref, wv_ref, k2_ref, v2_ref, k5_ref, v5_ref, *, nb):
    hb = _rms(x_ref[...], g_ref[...]).astype(BF16)
    for w_ref, o2_ref, o5_ref in ((wk_ref, k2_ref, k5_ref), (wv_ref, v2_ref, v5_ref)):
        y = _dot(hb, w_ref[...])
        o2_ref[...] = y.astype(o2_ref.dtype)
        for i in range(nb):
            for h in range(MEM_HEADS):
                o5_ref[i, :, h, :] = y[i * N_MEM:(i + 1) * N_MEM,
                                       h * MEM_HEAD_DIM:(h + 1) * MEM_HEAD_DIM]


def _mem_kv_proj(mem2d, g, wk, wv, *, layer, depth, k5, v5):
    m, k = mem2d.shape
    nb = ROW_BLOCK // N_MEM
    batch = m // N_MEM
    assert batch % nb == 0
    row_spec = pl.BlockSpec((nb * N_MEM, D_MODEL), lambda i: (i, 0))
    w_spec = pl.BlockSpec((k, D_MODEL), lambda i: (0, 0))
    out5 = pl.BlockSpec((None, nb, N_MEM, MEM_HEADS, MEM_HEAD_DIM), lambda i: (layer, i, 0, 0, 0))
    shape5 = jax.ShapeDtypeStruct((depth, batch, N_MEM, MEM_HEADS, MEM_HEAD_DIM), F32)
    return _call_with_carried(
        functools.partial(_mem_kv_body, nb=nb),
        grid=(batch // nb,),
        in_specs=[row_spec, pl.BlockSpec((1, k), lambda i: (0, 0)), w_spec, w_spec],
        inputs=[mem2d, g, wk, wv],
        out_specs=[row_spec, row_spec, out5, out5],
        out_shape=[jax.ShapeDtypeStruct((m, D_MODEL), BF16), jax.ShapeDtypeStruct((m, D_MODEL), BF16),
                   shape5, shape5],
        carried={2: k5, 3: v5}, semantics=("parallel",), name="mem_kv_proj")


def _conv_body(z_ref, buf_ref, w_ref, b_ref, lg_ref, lb_ref, y_ref, nc_ref, ext_ref, sh_ref,
               *, nb, tt):
    cur = CONV_BUF + (-CONV_BUF) % V7X_SUBLANES
    h0 = cur - CONV_BUF
    for i in range(nb):
        rows = slice(i * tt, (i + 1) * tt)
        if nb == 1:
            @pl.when(pl.program_id(1) == 0)
            def _():
                ext_ref[h0:cur, :] = buf_ref[0]
        else:
            ext_ref[h0:cur, :] = buf_ref[i]
        z = z_ref[rows, :]
        u = z[:, :CONV_CH] * jax.nn.sigmoid(z[:, CONV_CH:])
        ext_ref[cur:cur + tt, :] = u
        acc = None
        for r in range(V7X_SUBLANES):
            taps = range(r, CONV_WIDTH, V7X_SUBLANES)
            span = tt + taps[-1] - r
            sh_ref[r, 0:span, :] = ext_ref[h0 + r:h0 + r + span, :]
            for k in taps:
                term = sh_ref[r, k - r:k - r + tt, :] * w_ref[k:k + 1, :]
                acc = term if acc is None else acc + term
        y = acc + b_ref[...]
        mu = jnp.mean(y, axis=-1, keepdims=True)
        var = jnp.mean(jnp.square(y - mu), axis=-1, keepdims=True)
        yn = (y - mu) * lax.rsqrt(var + LN_EPS) * lg_ref[...] + lb_ref[...]
        y_ref[rows, :] = (yn * jax.nn.sigmoid(yn)).astype(y_ref.dtype)
        halo = ext_ref[tt + h0:tt + cur, :]
        if nb == 1:
            ext_ref[h0:cur, :] = halo
        nc_ref[i] = halo


def _conv_group(zc, row0, batch, seq, nb, tt, conv_buf, w, b, lg, lb, *, layer, depth, state,
                seq_major, name):
    nt = seq // tt
    rows = nb * tt
    base = row0 // rows
    assert seq % tt == 0 and batch % nb == 0 and row0 % rows == 0 and (nb == 1 or nt == 1)
    const = lambda bi, j: (0, 0)
    width = CONV_CH + RWKV_DIM
    if seq_major:
        assert nb == 1
        y_spec = pl.BlockSpec((None, tt, CONV_CH), lambda bi, j: (bi, j, 0))
        y_shape = jax.ShapeDtypeStruct((batch, seq, width), BF16)
    else:
        y_spec = pl.BlockSpec((rows, CONV_CH), lambda bi, j: (bi, 0))
        y_shape = jax.ShapeDtypeStruct((batch * seq, width), BF16)
    halo_rows = CONV_BUF + (-CONV_BUF) % V7X_SUBLANES
    return _call_with_carried(
        functools.partial(_conv_body, nb=nb, tt=tt),
        grid=(batch // nb, nt),
        in_specs=[pl.BlockSpec((rows, 2 * CONV_CH), lambda bi, j: (base + bi * nt + j, 0)),
                  pl.BlockSpec((nb, CONV_BUF, CONV_CH), lambda bi, j: (bi, 0, 0)),
                  pl.BlockSpec((CONV_WIDTH, CONV_CH), const),
                  pl.BlockSpec((1, CONV_CH), const), pl.BlockSpec((1, CONV_CH), const),
                  pl.BlockSpec((1, CONV_CH), const)],
        inputs=[zc, conv_buf, w, b, lg, lb],
        out_specs=[y_spec,
                   pl.BlockSpec((None, nb, CONV_BUF, CONV_CH), lambda bi, j: (layer, bi, 0, 0))],
        out_shape=[y_shape, jax.ShapeDtypeStruct((depth, batch, CONV_BUF, CONV_CH), F32)],
        carried={1: state},
        scratch_shapes=[pltpu.VMEM((halo_rows + tt, CONV_CH), F32),
                        pltpu.VMEM((V7X_SUBLANES, tt + CONV_BUF, CONV_CH), F32)],
        semantics=("parallel", "arbitrary"), name=name)


def _head_sums(xs, seg):
    rows = xs[0].shape[0]
    groups = RWKV_DIM // V7X_LANES
    stacked = jnp.concatenate(
        [x[:, g * V7X_LANES:(g + 1) * V7X_LANES] for x in xs for g in range(groups)], axis=0)
    hi, lo = _split2(stacked)
    y = _dot(hi, seg) + _dot(lo, seg)
    outs = []
    for i in range(len(xs)):
        parts = [y[(i * groups + g) * rows:(i * groups + g + 1) * rows] for g in range(groups)]
        outs.append(jnp.concatenate(parts, axis=1))
    return outs


def _block_diag(blocks):
    n = len(blocks)
    w = blocks[0].shape[1]
    rows = []
    for i, blk in enumerate(blocks):
        pieces = ([jnp.zeros((blk.shape[0], i * w), blk.dtype)] if i else []) + [blk]
        if i < n - 1:
            pieces.append(jnp.zeros((blk.shape[0], (n - 1 - i) * w), blk.dtype))
        rows.append(jnp.concatenate(pieces, axis=1))
    return jnp.concatenate(rows, axis=0)


def _rwkv_body(*refs, n_grp, spg, tlen, n_zr):
    zr_refs = refs[:n_zr]
    (zp0_ref, s0_ref, mu_ref, w0_ref, wup_ref, a0_ref, aup_ref, gup_ref, kk_ref, ka_ref, rk_ref,
     gng_ref, gnb_ref, seg_ref, o_ref, s_out_ref, s_scr, carry_scr) = refs[n_zr:]
    nb = n_grp * spg
    rows = nb * tlen
    gr = spg * tlen
    shift = int(math.log2(tlen))
    n_pack = RWKV_HEADS // PACK_HEADS
    pw = PACK_HEADS * RWKV_HEAD
    assert 1 << shift == tlen and gr == RWKV_HEAD == RWKV_GROUP_ROWS

    @pl.when(pl.program_id(1) == 0)
    def _():
        carry_scr[...] = zp0_ref[...]
        for q in range(nb):
            for p in range(n_pack):
                s_scr[q, p] = _block_diag([s0_ref[q, p * PACK_HEADS + h] for h in range(PACK_HEADS)])

    if n_zr == 1:
        zr = zr_refs[0][...]
    else:
        zr = jnp.concatenate([r[...] for r in zr_refs], axis=0)
    seqs = [slice(i * tlen, (i + 1) * tlen) for i in range(nb)]
    t_idx = lax.broadcasted_iota(jnp.int32, (rows, 1), 0) & (tlen - 1)
    if nb == 1:
        carry_rows = carry_scr[0]
    else:
        carry_rows = jnp.concatenate(
            [jnp.broadcast_to(carry_scr[i], (tlen, RWKV_COLS)) for i in range(nb)], axis=0)
    prev = jnp.where(t_idx == 0, carry_rows, pltpu.roll(zr, 1, axis=0))
    for i, rs in enumerate(seqs):
        carry_scr[i] = zr[rs.stop - 1:rs.stop, :]
    zs = zr + (prev - zr) * mu_ref[...]

    d = RWKV_DIM
    r = zs[:, 0:d]
    k = zs[:, d:2 * d]
    v = zs[:, 2 * d:3 * d]
    dw = zs[:, 3 * d:3 * d + DECAY_RANK]
    da = zs[:, 3 * d + DECAY_RANK:3 * d + DECAY_RANK + ICLR_RANK]
    dg = zs[:, 3 * d + DECAY_RANK + ICLR_RANK:]

    lw = w0_ref[...] + _mm(jnp.tanh(dw), wup_ref[...])
    w_raw = jnp.minimum(lw, 0.0) - jnp.log(1.0 + jnp.exp(-jnp.abs(lw))) - 0.5
    logw = -jnp.exp(w_raw)
    a = jax.nn.sigmoid(a0_ref[...] + _mm(da, aup_ref[...]))
    gate = _mm(jax.nn.sigmoid(dg), gup_ref[...])

    seg = seg_ref[...]
    kk = k * kk_ref[...]
    kmod = k * (1.0 + (a - 1.0) * ka_ref[...])
    kk_sq, bonus_dot = _head_sums([kk * kk, r * kmod * rk_ref[...]], seg)
    kk = kk / jnp.maximum(jnp.sqrt(kk_sq), 1e-12)
    bonus = bonus_dot * v

    ri = lax.broadcasted_iota(jnp.int32, (rows, rows), 0)
    ci = lax.broadcasted_iota(jnp.int32, (rows, rows), 1)
    tri = (((ri >> shift) == (ci >> shift)) & (ri >= ci)).astype(BF16)
    w_hi, w_mid, w_lo = _split3(logw)
    cl = _dot(tri, w_hi) + _dot(tri, w_mid) + _dot(tri, w_lo)
    g_incl = jnp.exp(cl)
    g_inv = jnp.exp(-cl)
    a_t = -kk * jnp.exp(cl - logw)
    b_t = kk * a * g_inv
    k_t = kmod * g_inv
    r_t = r * g_incl

    pr = lax.broadcasted_iota(jnp.int32, (gr, pw), 0)
    pc = lax.broadcasted_iota(jnp.int32, (gr, pw), 1) & (RWKV_HEAD - 1)
    p_same = (pr >> shift) == (pc >> shift)
    p_incl = p_same & (pr >= pc)
    p_strict = p_same & (pr > pc)
    p_eye = (pr == pc).astype(F32)
    base_shift = min(shift, int(math.log2(INV_BASE)))
    base_blk = (pr >> base_shift) == (pc >> base_shift)
    level_blks = [((pr >> (lv + 1)) == (pc >> (lv + 1))) & ((pr >> lv) != (pc >> lv))
                  for lv in range(base_shift, shift)]
    bd_mask = ((lax.broadcasted_iota(jnp.int32, (pw, pw), 0) >> 6)
               == (lax.broadcasted_iota(jnp.int32, (pw, pw), 1) >> 6))
    bd_zero = jnp.zeros((pw, pw), BF16)

    def bd(x):
        return jnp.where(bd_mask, jnp.concatenate([x.astype(BF16)] * PACK_HEADS, axis=0), bd_zero)

    def pk(x, y_bd, dims=None):
        return _mm(x, y_bd, dims)

    chains = [(g, p) for g in range(n_grp) for p in range(n_pack)]
    nch = range(len(chains))
    cut = lambda m, g, p: m[g * gr:(g + 1) * gr, p * pw:(p + 1) * pw]
    a_p = [cut(a_t, g, p) for g, p in chains]
    b_p = [cut(b_t, g, p) for g, p in chains]
    k_p = [cut(k_t, g, p) for g, p in chains]
    r_p = [cut(r_t, g, p) for g, p in chains]
    v_p = [cut(v, g, p) for g, p in chains]
    sub = [slice(j * tlen, (j + 1) * tlen) for j in range(spg)]
    ar = [jnp.concatenate([a_p[c], r_p[c]], axis=0) for c in nch]
    b_bd = [bd(b_p[c]) for c in nch]
    k_bd = [bd(k_p[c]) for c in nch]
    v_bd = [bd(v_p[c]) for c in nch]
    gb = [pk(ar[c], b_bd[c], _NT) for c in nch]
    gk = [pk(ar[c], k_bd[c], _NT) for c in nch]
    a_ab = [jnp.where(p_strict, gb[c][:gr], 0.0) for c in nch]
    a_rb = [jnp.where(p_incl, gb[c][gr:], 0.0) for c in nch]
    a_ak = [jnp.where(p_strict, gk[c][:gr], 0.0) for c in nch]
    a_rk = [jnp.where(p_incl, gk[c][gr:], 0.0) for c in nch]
    s_old = [[s_scr[g * spg + j, p] for j in range(spg)] for g, p in chains]
    s_terms = [[pk(jnp.concatenate([a_p[c][sr], r_p[c][sr]], axis=0), s_old[c][j], _NT)
                for j, sr in enumerate(sub)] for c in nch]
    a_s0 = [jnp.concatenate([x[:tlen] for x in s_terms[c]], axis=0) for c in nch]
    r_s0 = [jnp.concatenate([x[tlen:] for x in s_terms[c]], axis=0) for c in nch]
    av = [pk(jnp.concatenate([a_ak[c], a_rk[c]], axis=0), v_bd[c]) for c in nch]
    rhs = [a_s0[c] + av[c][:gr] for c in nch]
    npow = [jnp.where(base_blk, a_ab[c], 0.0) for c in nch]
    inv = [p_eye + npow[c] for c in nch]
    for _ in range(max(0, base_shift - 1)):
        npow = [pk(npow[c], bd(npow[c])) for c in nch]
        inv = [inv[c] + pk(inv[c], bd(npow[c])) for c in nch]
    for off_blk in level_blks:
        cross = [pk(jnp.where(off_blk, a_ab[c], 0.0), bd(inv[c])) for c in nch]
        inv = [inv[c] + pk(inv[c], bd(cross[c])) for c in nch]
    u = [pk(inv[c], bd(rhs[c])) for c in nch]
    outs = [r_s0[c] + pk(a_rb[c], bd(u[c])) + av[c][gr:] for c in nch]
    for c, (g, p) in enumerate(chains):
        for j, sr in enumerate(sub):
            uv = jnp.concatenate([u[c][sr], v_p[c][sr]], axis=0)
            bk = jnp.concatenate([b_p[c][sr], k_p[c][sr]], axis=0)
            last_row = g * gr + sr.stop - 1
            g_last = g_incl[last_row:last_row + 1, p * pw:(p + 1) * pw]
            upd = jnp.where(bd_mask, pk(uv, bk, _TN), 0.0)
            s_scr[g * spg + j, p] = (s_old[c][j] + upd) * g_last
    o = jnp.concatenate(
        [jnp.concatenate([outs[g * n_pack + p] for p in range(n_pack)], axis=1)
         for g in range(n_grp)], axis=0)

    inv_n = 1.0 / RWKV_HEAD
    (o_sum,) = _head_sums([o], seg)
    cen = o - o_sum * inv_n
    (sq_sum,) = _head_sums([cen * cen], seg)
    y = cen * lax.rsqrt(sq_sum * inv_n + GN_EPS) * gng_ref[...] + gnb_ref[...] + bonus
    y = (y * gate).astype(o_ref.dtype)
    if len(o_ref.shape) == 3:
        for i, rs in enumerate(seqs):
            o_ref[i] = y[rs]
    else:
        o_ref[...] = y

    @pl.when(pl.program_id(1) == pl.num_programs(1) - 1)
    def _():
        for q in range(nb):
            for h in range(RWKV_HEADS):
                blk = slice((h % PACK_HEADS) * RWKV_HEAD, (h % PACK_HEADS + 1) * RWKV_HEAD)
                s_out_ref[q, h] = s_scr[q, h // PACK_HEADS, blk, blk]


def _rwkv_group(zr, row0, batch, seq, n_grp, spg, tlen, zp0, s0, p, seg, *, layer, depth, mixed,
                state, name):
    nb = n_grp * spg
    nc = seq // tlen
    assert seq % tlen == 0 and batch % nb == 0 and row0 % tlen == 0 and (spg == 1 or nc == 1)
    const = lambda bi, j: (0, 0)
    vec = lambda n: pl.BlockSpec((1, n), const)
    mat = lambda a: pl.BlockSpec(a.shape, const)
    st = (nb, RWKV_HEADS, RWKV_HEAD, RWKV_HEAD)
    base = row0 // tlen
    if nc > 1:
        zr_specs = [pl.BlockSpec((tlen, RWKV_COLS),
                                 lambda bi, j, i=i: (base + (bi * nb + i) * nc + j, 0))
                    for i in range(nb)]
    else:
        assert row0 % (nb * tlen) == 0
        zr_specs = [pl.BlockSpec((nb * tlen, RWKV_COLS),
                                 lambda bi, j: (row0 // (nb * tlen) + bi, 0))]
    if mixed.ndim == 3:
        o_spec = pl.BlockSpec((nb, tlen, RWKV_DIM), lambda bi, j: (bi, j, 1))
    else:
        assert nc == 1
        o_spec = pl.BlockSpec((nb * tlen, RWKV_DIM), lambda bi, j: (bi, 1))
    n_zr = len(zr_specs)
    return _call_with_carried(
        functools.partial(_rwkv_body, n_grp=n_grp, spg=spg, tlen=tlen, n_zr=n_zr),
        grid=(batch // nb, nc),
        in_specs=zr_specs + [
            pl.BlockSpec((nb, 1, RWKV_COLS), lambda bi, j: (bi, 0, 0)),
            pl.BlockSpec(st, lambda bi, j: (bi, 0, 0, 0)),
            vec(RWKV_COLS), vec(RWKV_DIM), mat(p["w_up"]), vec(RWKV_DIM), mat(p["a_up"]),
            mat(p["g_up"]), vec(RWKV_DIM), vec(RWKV_DIM), vec(RWKV_DIM), vec(RWKV_DIM),
            vec(RWKV_DIM), mat(seg)],
        inputs=[zr] * n_zr + [zp0, s0, p["mu"], p["w0"], p["w_up"], p["a0"], p["a_up"], p["g_up"],
                              p["k_k"], p["k_a"], p["r_k"], p["gn_g"], p["gn_b"], seg],
        out_specs=[o_spec, pl.BlockSpec((None,) + st, lambda bi, j: (layer, bi, 0, 0, 0))],
        out_shape=[jax.ShapeDtypeStruct(mixed.shape, BF16),
                   jax.ShapeDtypeStruct((depth, batch) + st[1:], F32)],
        carried={0: mixed, 1: state},
        scratch_shapes=[pltpu.VMEM((nb, RWKV_HEADS // PACK_HEADS, PACK_HEADS * RWKV_HEAD,
                                    PACK_HEADS * RWKV_HEAD), F32),
                        pltpu.VMEM((nb, 1, RWKV_COLS), F32)],
        semantics=("parallel", "arbitrary"), name=name)


def _attn_body(q_ref, k_ref, v_ref, o_ref, *dense_refs, bb, tq):
    scale = MEM_HEAD_DIM ** -0.5
    pairs = [(i, h) for i in range(bb) for h in range(MEM_HEADS)]
    rows = lambda i: slice(i * tq, (i + 1) * tq)
    cols = lambda h: slice(h * MEM_HEAD_DIM, (h + 1) * MEM_HEAD_DIM)

    if dense_refs:
        for src, dst in zip((k_ref, v_ref), dense_refs):
            for i, h in pairs:
                dst[i, :, cols(h)] = src[i, :, h, :]
        k_ref, v_ref = dense_refs

    def head(ref, i, h):
        return ref[i, :, cols(h)].astype(BF16)

    s = [lax.dot_general(q_ref[rows(i), cols(h)], head(k_ref, i, h), _NT,
                         preferred_element_type=F32) * scale for i, h in pairs]
    e = [jnp.exp(x - jnp.max(x, axis=-1, keepdims=True)) for x in s]
    p = [x / jnp.sum(x, axis=-1, keepdims=True) for x in e]
    o = [_dot(p[n].astype(BF16), head(v_ref, i, h)) for n, (i, h) in enumerate(pairs)]
    for n, (i, h) in enumerate(pairs):
        o_ref[rows(i), cols(h)] = o[n].astype(o_ref.dtype)


def _attn_group(q, row0, batch, seq, tq, bb, mk, mv, *, layer, name):
    nt = seq // tq
    rows = bb * tq
    base = row0 // rows
    assert seq % tq == 0 and batch % bb == 0 and row0 % rows == 0 and (bb == 1 or nt == 1)
    if mk.ndim == 5:
        kv_spec = pl.BlockSpec((None, bb, N_MEM, MEM_HEADS, MEM_HEAD_DIM),
                               lambda bi, j: (layer, bi, 0, 0, 0))
        scratch = [pltpu.VMEM((bb, N_MEM, D_MODEL), F32)] * 2
    else:
        kv_spec = pl.BlockSpec((bb, N_MEM, D_MODEL), lambda bi, j: (bi, 0, 0))
        scratch = []
    return pl.pallas_call(
        functools.partial(_attn_body, bb=bb, tq=tq),
        grid=(batch // bb, nt),
        in_specs=[pl.BlockSpec((rows, D_MODEL), lambda bi, j: (base + bi * nt + j, 0)),
                  kv_spec, kv_spec],
        out_specs=pl.BlockSpec((rows, D_MODEL), lambda bi, j: (bi * nt + j, 0)),
        out_shape=jax.ShapeDtypeStruct((batch * seq, D_MODEL), BF16),
        scratch_shapes=scratch,
        compiler_params=_params(("parallel", "arbitrary")), name=name,
    )(q, mk, mv)


def kernel(x_prompt, x_sample, cache_mem_k, cache_mem_v, state_wkv, state_conv, state_shift,
           mem_prompt, w_in, mu_shift, conv_w, conv_b, conv_ln_g, conv_ln_b, rwkv_w0, rwkv_w_up,
           rwkv_a0, rwkv_a_up, rwkv_g_up, rwkv_k_k, rwkv_k_a, rwkv_r_k, rwkv_gn_g, rwkv_gn_b,
           w_out, mem_norm_g, w_q, w_k, w_v, w_o, w_ffn1, w_ffn2, norm_g):
    bp, tp, _ = x_prompt.shape
    bs, ts, _ = x_sample.shape
    depth = w_in.shape[0]
    n_p, n_s = bp * tp, bs * ts
    row = lambda a: a.reshape(1, -1)
    sample_nb = SAMPLE_STEP_ROWS // ts
    prompt_nb = min(RWKV_PROMPT_SEQS, bp)

    x = jnp.concatenate([x_prompt.reshape(n_p, D_MODEL), x_sample.reshape(n_s, D_MODEL)], axis=0)
    mem2d = mem_prompt.reshape(bp * N_MEM, D_MODEL)
    lane_head = jnp.arange(V7X_LANES) // RWKV_HEAD
    seg = (lane_head[:, None] == lane_head[None, :]).astype(BF16)
    zero_conv = jnp.zeros((bp, CONV_BUF, CONV_CH), F32)
    zero_prev = jnp.zeros((bp, 1, RWKV_COLS), F32)
    zero_wkv = jnp.zeros((bp, RWKV_HEADS, RWKV_HEAD, RWKV_HEAD), F32)

    in_splits = [(0, 2 * CONV_CH), (2 * CONV_CH, IN_COLS)]
    w_in_layers = [w_in[l].astype(BF16) for l in range(depth)]
    zc, zr = _norm_matmul(x, norm_g[0][0:1], w_in_layers[0], in_splits, out_dtype=F32, name="in_proj")

    shift_p, shift_s = [], []
    mk5 = mv5 = wkv_p = conv_p = wkv_s = conv_s = None
    for l in range(depth):
        g = norm_g[l]
        w_in_b = w_in_layers[l]
        rp = dict(mu=row(mu_shift[l]), w0=row(rwkv_w0[l]), w_up=rwkv_w_up[l].astype(BF16),
                  a0=row(rwkv_a0[l]), a_up=rwkv_a_up[l].astype(BF16),
                  g_up=rwkv_g_up[l].astype(BF16), k_k=row(rwkv_k_k[l]), k_a=row(rwkv_k_a[l]),
                  r_k=row(rwkv_r_k[l]), gn_g=row(rwkv_gn_g[l]), gn_b=row(rwkv_gn_b[l]))
        cw, cb = conv_w[l], row(conv_b[l])
        clg, clb = row(conv_ln_g[l]), row(conv_ln_b[l])

        last = jnp.concatenate([lax.slice(x, (tp - 1, 0), (n_p, D_MODEL), (tp, 1)),
                                lax.slice(x, (n_p + ts - 1, 0), (n_p + n_s, D_MODEL), (ts, 1))],
                               axis=0)
        shift = _rms_rows(last, g[0:1])
        shift_p.append(shift[:bp])
        shift_s.append(shift[bp:])

        (zp_s,) = _norm_matmul(state_shift[l], g[0:1], w_in_b, in_splits[1:], out_dtype=F32,
                               normalize=False, name="prev_proj")

        mixed_p, conv_p = _conv_group(zc, 0, bp, tp, 1, min(CONV_TILE, tp), zero_conv, cw, cb,
                                      clg, clb, layer=l, depth=depth, state=conv_p,
                                      seq_major=True, name="conv_prompt")
        mixed_s, conv_s = _conv_group(zc, n_p, bs, ts, sample_nb, ts, state_conv[l], cw, cb, clg,
                                      clb, layer=l, depth=depth, state=conv_s, seq_major=False,
                                      name="conv_sample")
        mixed_p, wkv_p = _rwkv_group(zr, 0, bp, tp, prompt_nb, 1, RWKV_GROUP_ROWS, zero_prev, zero_wkv,
                                     rp, seg, layer=l, depth=depth, mixed=mixed_p, state=wkv_p,
                                     name="rwkv_prompt")
        mixed_s, wkv_s = _rwkv_group(zr, n_p, bs, ts, 1, RWKV_GROUP_ROWS // ts, ts, zp_s[:, None, :], state_wkv[l],
                                     rp, seg, layer=l, depth=depth, mixed=mixed_s, state=wkv_s,
                                     name="rwkv_sample")

        x, q = _res_proj([mixed_p.reshape(n_p, D_MODEL), mixed_s], w_out[l].astype(BF16), x, g[1:2],
                         g_b=g[2:3], w_b=w_q[l].astype(BF16), splits=[(0, D_MODEL)], out_dtype=BF16,
                         name="out_q_proj")
        mk2, mv2, mk5, mv5 = _mem_kv_proj(mem2d, row(mem_norm_g), w_k[l].astype(BF16),
                                          w_v[l].astype(BF16), layer=l, depth=depth, k5=mk5, v5=mv5)
        att_p = _attn_group(q, 0, bp, tp, min(ATTN_TILE, tp), 1, mk2.reshape(bp, N_MEM, D_MODEL),
                            mv2.reshape(bp, N_MEM, D_MODEL), layer=l, name="attn_prompt")
        att_s = _attn_group(q, n_p, bs, ts, ts, ATTN_SAMPLE_BATCH, cache_mem_k, cache_mem_v,
                            layer=l, name="attn_sample")
        x, f = _res_proj([att_p, att_s], w_o[l].astype(BF16), x, g[3:4], g_b=g[4:5],
                         w_b=w_ffn1[l].astype(BF16), splits=[(0, D_FF)], relu2=True, out_dtype=BF16,
                         name="attn_out_ffn_up")
        w_down = w_ffn2[l].astype(BF16)
        if l + 1 < depth:
            x, zc, zr = _res_proj([f], w_down, x, g[5:6], g_b=norm_g[l + 1][0:1],
                                  w_b=w_in_layers[l + 1], splits=in_splits, out_dtype=F32,
                                  name="ffn_down_in_proj")
        else:
            (y_p,) = _res_proj([f], w_down, x, g[5:6], row0=0, n_rows=n_p, name="ffn_down_prompt")
            (y_s,) = _res_proj([f], w_down, x, g[5:6], row0=n_p, n_rows=n_s, name="ffn_down_sample")

    return (y_p.reshape(bp, tp, D_MODEL), y_s.reshape(bs, ts, D_MODEL), mk5, mv5, wkv_p, conv_p,
            jnp.stack(shift_p), wkv_s, conv_s, jnp.stack(shift_s))
```

```python
import functools
import math

import jax
import jax.numpy as jnp
from jax import lax
from jax.experimental import pallas as pl
from jax.experimental.pallas import tpu as pltpu

F32 = jnp.float32
BF16 = jnp.bfloat16

D_MODEL = 1024
CONV_CH = 512
RWKV_DIM = 512
RWKV_HEAD = 64
RWKV_HEADS = 8
CONV_WIDTH = 31
CONV_BUF = CONV_WIDTH - 1
DECAY_RANK = 64
ICLR_RANK = 64
GATE_RANK = 128
RWKV_COLS = 3 * RWKV_DIM + DECAY_RANK + ICLR_RANK + GATE_RANK
IN_COLS = 2 * CONV_CH + RWKV_COLS
N_MEM = 256
MEM_HEADS = 4
MEM_HEAD_DIM = D_MODEL // MEM_HEADS
D_FF = 4 * D_MODEL
RMS_EPS = 1e-6
LN_EPS = 1e-5
GN_EPS = 64e-5

V7X_LANES = 128
V7X_SUBLANES = 8
V7X_VMEM_LIMIT_BYTES = 56 * 1024 * 1024

ROW_BLOCK = 512
COL_CHUNK = 1024
CONV_TILE = 256
RWKV_GROUP_ROWS = 64
RWKV_PROMPT_SEQS = 8
PACK_HEADS = 4
SAMPLE_STEP_ROWS = 64
INV_BASE = 8
ATTN_TILE = 512
ATTN_SAMPLE_BATCH = 4

_NT = (((1,), (1,)), ((), ()))
_TN = (((0,), (0,)), ((), ()))


def _params(semantics):
    return pltpu.CompilerParams(dimension_semantics=semantics,
                                vmem_limit_bytes=V7X_VMEM_LIMIT_BYTES)


def _dot(a, b):
    return jnp.dot(a, b, preferred_element_type=F32)


def _mm(a, b, dims=None):
    if dims is None:
        dims = (((a.ndim - 1,), (0,)), ((), ()))
    return lax.dot_general(a.astype(BF16), b.astype(BF16), dims, preferred_element_type=F32)


def _split2(x):
    hi = x.astype(BF16)
    lo = (x - hi.astype(F32)).astype(BF16)
    return hi, lo


def _split3(x):
    hi = x.astype(BF16)
    r1 = x - hi.astype(F32)
    mid = r1.astype(BF16)
    lo = (r1 - mid.astype(F32)).astype(BF16)
    return hi, mid, lo


def _rms(x, g):
    ms = jnp.mean(x * x, axis=-1, keepdims=True)
    return x * lax.rsqrt(ms + RMS_EPS) * g


def _drop_carried(body, n_in, n_carried):
    def wrapped(*refs):
        return body(*refs[:n_in], *refs[n_in + n_carried:])
    return wrapped


def _call_with_carried(body, *, grid, in_specs, inputs, out_specs, out_shape, carried,
                       scratch_shapes=(), semantics, name):
    n_in = len(inputs)
    extra, aliases = [], {}
    for out_idx, buf in sorted(carried.items()):
        if buf is None:
            continue
        aliases[n_in + len(extra)] = out_idx
        extra.append(buf)
    return pl.pallas_call(
        _drop_carried(body, n_in, len(extra)),
        grid=grid,
        in_specs=list(in_specs) + [pl.BlockSpec(memory_space=pl.ANY)] * len(extra),
        out_specs=out_specs, out_shape=out_shape, scratch_shapes=list(scratch_shapes),
        input_output_aliases=aliases, compiler_params=_params(semantics), name=name,
    )(*inputs, *extra)


def _resident(shape):
    return pl.BlockSpec(shape, lambda i: (0,) * len(shape), pipeline_mode=pl.Buffered(1))


def _project(hb, w_ref, o_refs, splits, relu2):
    for (lo, hi), o_ref in zip(splits, o_refs):
        for c0 in range(lo, hi, COL_CHUNK):
            c1 = min(hi, c0 + COL_CHUNK)
            y = _dot(hb, w_ref[:, c0:c1])
            if relu2:
                y = jnp.square(jnp.maximum(y, 0.0))
            o_ref[:, c0 - lo:c1 - lo] = y.astype(o_ref.dtype)


def _norm_matmul_body(x_ref, g_ref, w_ref, *o_refs, splits, normalize, relu2):
    x = x_ref[...]
    h = _rms(x, g_ref[...]) if normalize else x
    _project(h.astype(BF16), w_ref, o_refs, splits, relu2)


def _norm_matmul(x, g, w, splits, *, out_dtype, normalize=True, relu2=False, name):
    m, k = x.shape
    bm = math.gcd(ROW_BLOCK, m)
    return pl.pallas_call(
        functools.partial(_norm_matmul_body, splits=tuple(splits), normalize=normalize,
                          relu2=relu2),
        grid=(m // bm,),
        in_specs=[pl.BlockSpec((bm, k), lambda i: (i, 0)), _resident((1, k)), _resident(w.shape)],
        out_specs=[pl.BlockSpec((bm, hi - lo), lambda i: (i, 0)) for lo, hi in splits],
        out_shape=[jax.ShapeDtypeStruct((m, hi - lo), out_dtype) for lo, hi in splits],
        compiler_params=_params(("parallel",)), name=name,
    )(x, g, w)


def _res_proj_body(*refs, n_lhs, first_blocks, splits, relu2):
    y_refs = refs[:n_lhs]
    w_a_ref, x_ref, g_a_ref = refs[n_lhs:n_lhs + 3]
    rest = refs[n_lhs + 3:]
    if splits:
        g_b_ref, w_b_ref = rest[:2]
        rest = rest[2:]
    o_ref, p_refs = rest[0], rest[1:]

    def emit(y_ref):
        x_new = x_ref[...] + _rms(_dot(y_ref[...], w_a_ref[...]), g_a_ref[...])
        o_ref[...] = x_new
        if splits:
            _project(_rms(x_new, g_b_ref[...]).astype(BF16), w_b_ref, p_refs, splits, relu2)

    if n_lhs == 1:
        emit(y_refs[0])
    else:
        i = pl.program_id(0)
        pl.when(i < first_blocks)(lambda: emit(y_refs[0]))
        pl.when(i >= first_blocks)(lambda: emit(y_refs[1]))


def _res_proj(lhs, w_a, x, g_a, *, g_b=None, w_b=None, splits=(), relu2=False, out_dtype=None,
              row0=0, n_rows=None, name):
    m, n = x.shape
    k = w_a.shape[0]
    n_rows = m if n_rows is None else n_rows
    bm = math.gcd(ROW_BLOCK, n_rows, row0, *[a.shape[0] for a in lhs])
    if len(lhs) == 1:
        lhs_specs = [pl.BlockSpec((bm, k), lambda i: (row0 // bm + i, 0))]
        first_blocks = None
    else:
        assert row0 == 0 and n_rows == m and sum(a.shape[0] for a in lhs) == m
        first_blocks = lhs[0].shape[0] // bm
        lhs_specs = [pl.BlockSpec((bm, k), lambda i: (jnp.minimum(i, first_blocks - 1), 0)),
                     pl.BlockSpec((bm, k), lambda i: (jnp.maximum(i - first_blocks, 0), 0))]
    splits = tuple(splits)
    row_out = lambda width: pl.BlockSpec((bm, width), lambda i: (i, 0))
    proj_in = [_resident((1, n)), _resident(w_b.shape)] if splits else []
    proj_args = [g_b, w_b] if splits else []
    return pl.pallas_call(
        functools.partial(_res_proj_body, n_lhs=len(lhs), first_blocks=first_blocks,
                          splits=splits, relu2=relu2),
        grid=(n_rows // bm,),
        in_specs=lhs_specs + [_resident((k, n)),
                              pl.BlockSpec((bm, n), lambda i: (row0 // bm + i, 0)),
                              _resident((1, n))] + proj_in,
        out_specs=[row_out(n)] + [row_out(hi - lo) for lo, hi in splits],
        out_shape=[jax.ShapeDtypeStruct((n_rows, n), F32)]
        + [jax.ShapeDtypeStruct((n_rows, hi - lo), out_dtype) for lo, hi in splits],
        compiler_params=_params(("parallel",)), name=name,
    )(*lhs, w_a, x, g_a, *proj_args)


def _rms_rows_body(x_ref, g_ref, o_ref):
    o_ref[...] = _rms(x_ref[...], g_ref[...])


def _rms_rows(x, g, row0, stride, n_blocks, bm, *, name):
    k = x.shape[1]
    assert row0 % bm == 0 and stride % bm == 0
    return pl.pallas_call(
        _rms_rows_body, grid=(n_blocks,),
        in_specs=[pl.BlockSpec((bm, k), lambda i: (row0 // bm + i * (stride // bm), 0)),
                  _resident((1, k))],
        out_specs=pl.BlockSpec((bm, k), lambda i: (i, 0)),
        out_shape=jax.ShapeDtypeStruct((n_blocks * bm, k), F32),
        compiler_params=_params(("parallel",)), name=name,
    )(x, g)


def _mem_kv_body(x_ref, g_ref, wk_ref, wv_ref, k2_ref, v2_ref, k5_ref, v5_ref, *, nb):
    hb = _rms(x_ref[...], g_ref[...]).astype(BF16)
    for w_ref, o2_ref, o5_ref in ((wk_ref, k2_ref, k5_ref), (wv_ref, v2_ref, v5_ref)):
        y = _dot(hb, w_ref[...])
        o2_ref[...] = y.astype(o2_ref.dtype)
        for i in range(nb):
            for h in range(MEM_HEADS):
                o5_ref[i, :, h, :] = y[i * N_MEM:(i + 1) * N_MEM,
                                       h * MEM_HEAD_DIM:(h + 1) * MEM_HEAD_DIM]


def _mem_kv_proj(mem2d, g, wk, wv, *, layer, depth, k5, v5):
    m, k = mem2d.shape
    nb = ROW_BLOCK // N_MEM
    batch = m // N_MEM
    assert batch % nb == 0
    row_spec = pl.BlockSpec((nb * N_MEM, D_MODEL), lambda i: (i, 0))
    w_spec = pl.BlockSpec((k, D_MODEL), lambda i: (0, 0))
    out5 = pl.BlockSpec((None, nb, N_MEM, MEM_HEADS, MEM_HEAD_DIM), lambda i: (layer, i, 0, 0, 0))
    shape5 = jax.ShapeDtypeStruct((depth, batch, N_MEM, MEM_HEADS, MEM_HEAD_DIM), F32)
    return _call_with_carried(
        functools.partial(_mem_kv_body, nb=nb),
        grid=(batch // nb,),
        in_specs=[row_spec, pl.BlockSpec((1, k), lambda i: (0, 0)), w_spec, w_spec],
        inputs=[mem2d, g, wk, wv],
        out_specs=[row_spec, row_spec, out5, out5],
        out_shape=[jax.ShapeDtypeStruct((m, D_MODEL), BF16), jax.ShapeDtypeStruct((m, D_MODEL), BF16),
                   shape5, shape5],
        carried={2: k5, 3: v5}, semantics=("parallel",), name="mem_kv_proj")


def _conv_body(z_ref, buf_ref, w_ref, b_ref, lg_ref, lb_ref, y_ref, nc_ref, ext_ref, sh_ref,
               *, nb, tt):
    cur = CONV_BUF + (-CONV_BUF) % V7X_SUBLANES
    h0 = cur - CONV_BUF
    for i in range(nb):
        rows = slice(i * tt, (i + 1) * tt)
        if nb == 1:
            @pl.when(pl.program_id(1) == 0)
            def _():
                ext_ref[h0:cur, :] = buf_ref[0]
        else:
            ext_ref[h0:cur, :] = buf_ref[i]
        z = z_ref[rows, :]
        u = z[:, :CONV_CH] * jax.nn.sigmoid(z[:, CONV_CH:])
        ext_ref[cur:cur + tt, :] = u
        acc = None
        for r in range(V7X_SUBLANES):
            taps = range(r, CONV_WIDTH, V7X_SUBLANES)
            span = tt + taps[-1] - r
            sh_ref[r, 0:span, :] = ext_ref[h0 + r:h0 + r + span, :]
            for k in taps:
                term = sh_ref[r, k - r:k - r + tt, :] * w_ref[k:k + 1, :]
                acc = term if acc is None else acc + term
        y = acc + b_ref[...]
        mu = jnp.mean(y, axis=-1, keepdims=True)
        var = jnp.mean(jnp.square(y - mu), axis=-1, keepdims=True)
        yn = (y - mu) * lax.rsqrt(var + LN_EPS) * lg_ref[...] + lb_ref[...]
        y_ref[rows, :] = (yn * jax.nn.sigmoid(yn)).astype(y_ref.dtype)
        halo = ext_ref[tt + h0:tt + cur, :]
        if nb == 1:
            ext_ref[h0:cur, :] = halo
        nc_ref[i] = halo


def _conv_group(zc, row0, batch, seq, nb, tt, conv_buf, w, b, lg, lb, *, layer, depth, state,
                seq_major, name):
    nt = seq // tt
    rows = nb * tt
    base = row0 // rows
    assert seq % tt == 0 and batch % nb == 0 and row0 % rows == 0 and (nb == 1 or nt == 1)
    const = lambda bi, j: (0, 0)
    width = CONV_CH + RWKV_DIM
    if seq_major:
        assert nb == 1
        y_spec = pl.BlockSpec((None, tt, CONV_CH), lambda bi, j: (bi, j, 0))
        y_shape = jax.ShapeDtypeStruct((batch, seq, width), BF16)
    else:
        y_spec = pl.BlockSpec((rows, CONV_CH), lambda bi, j: (bi, 0))
        y_shape = jax.ShapeDtypeStruct((batch * seq, width), BF16)
    halo_rows = CONV_BUF + (-CONV_BUF) % V7X_SUBLANES
    return _call_with_carried(
        functools.partial(_conv_body, nb=nb, tt=tt),
        grid=(batch // nb, nt),
        in_specs=[pl.BlockSpec((rows, 2 * CONV_CH), lambda bi, j: (base + bi * nt + j, 0)),
                  pl.BlockSpec((nb, CONV_BUF, CONV_CH), lambda bi, j: (bi, 0, 0)),
                  pl.BlockSpec((CONV_WIDTH, CONV_CH), const),
                  pl.BlockSpec((1, CONV_CH), const), pl.BlockSpec((1, CONV_CH), const),
                  pl.BlockSpec((1, CONV_CH), const)],
        inputs=[zc, conv_buf, w, b, lg, lb],
        out_specs=[y_spec,
                   pl.BlockSpec((None, nb, CONV_BUF, CONV_CH), lambda bi, j: (layer, bi, 0, 0))],
        out_shape=[y_shape, jax.ShapeDtypeStruct((depth, batch, CONV_BUF, CONV_CH), F32)],
        carried={1: state},
        scratch_shapes=[pltpu.VMEM((halo_rows + tt, CONV_CH), F32),
                        pltpu.VMEM((V7X_SUBLANES, tt + CONV_BUF, CONV_CH), F32)],
        semantics=("parallel", "arbitrary"), name=name)


def _head_sums(xs, seg):
    rows = xs[0].shape[0]
    groups = RWKV_DIM // V7X_LANES
    stacked = jnp.concatenate(
        [x[:, g * V7X_LANES:(g + 1) * V7X_LANES] for x in xs for g in range(groups)], axis=0)
    hi, lo = _split2(stacked)
    y = _dot(hi, seg) + _dot(lo, seg)
    outs = []
    for i in range(len(xs)):
        parts = [y[(i * groups + g) * rows:(i * groups + g + 1) * rows] for g in range(groups)]
        outs.append(jnp.concatenate(parts, axis=1))
    return outs


def _block_diag(blocks):
    n = len(blocks)
    w = blocks[0].shape[1]
    rows = []
    for i, blk in enumerate(blocks):
        pieces = ([jnp.zeros((blk.shape[0], i * w), blk.dtype)] if i else []) + [blk]
        if i < n - 1:
            pieces.append(jnp.zeros((blk.shape[0], (n - 1 - i) * w), blk.dtype))
        rows.append(jnp.concatenate(pieces, axis=1))
    return jnp.concatenate(rows, axis=0)


def _rwkv_body(*refs, n_grp, spg, tlen, n_zr):
    zr_refs = refs[:n_zr]
    (zp0_ref, s0_ref, mu_ref, w0_ref, wup_ref, a0_ref, aup_ref, gup_ref, kk_ref, ka_ref, rk_ref,
     gng_ref, gnb_ref, seg_ref, o_ref, s_out_ref, s_scr, carry_scr) = refs[n_zr:]
    nb = n_grp * spg
    rows = nb * tlen
    gr = spg * tlen
    shift = int(math.log2(tlen))
    n_pack = RWKV_HEADS // PACK_HEADS
    pw = PACK_HEADS * RWKV_HEAD
    assert 1 << shift == tlen and gr == RWKV_HEAD == RWKV_GROUP_ROWS

    @pl.when(pl.program_id(1) == 0)
    def _():
        carry_scr[...] = zp0_ref[...]
        for q in range(nb):
            for p in range(n_pack):
                s_scr[q, p] = _block_diag([s0_ref[q, p * PACK_HEADS + h] for h in range(PACK_HEADS)])

    if n_zr == 1:
        zr = zr_refs[0][...]
    else:
        zr = jnp.concatenate([r[...] for r in zr_refs], axis=0)
    seqs = [slice(i * tlen, (i + 1) * tlen) for i in range(nb)]
    t_idx = lax.broadcasted_iota(jnp.int32, (rows, 1), 0) & (tlen - 1)
    if nb == 1:
        carry_rows = carry_scr[0]
    else:
        carry_rows = jnp.concatenate(
            [jnp.broadcast_to(carry_scr[i], (tlen, RWKV_COLS)) for i in range(nb)], axis=0)
    prev = jnp.where(t_idx == 0, carry_rows, pltpu.roll(zr, 1, axis=0))
    for i, rs in enumerate(seqs):
        carry_scr[i] = zr[rs.stop - 1:rs.stop, :]
    zs = zr + (prev - zr) * mu_ref[...]

    d = RWKV_DIM
    r = zs[:, 0:d]
    k = zs[:, d:2 * d]
    v = zs[:, 2 * d:3 * d]
    dw = zs[:, 3 * d:3 * d + DECAY_RANK]
    da = zs[:, 3 * d + DECAY_RANK:3 * d + DECAY_RANK + ICLR_RANK]
    dg = zs[:, 3 * d + DECAY_RANK + ICLR_RANK:]

    lw = w0_ref[...] + _mm(jnp.tanh(dw), wup_ref[...])
    logw = -math.exp(-0.5) * jax.nn.sigmoid(lw)
    a = jax.nn.sigmoid(a0_ref[...] + _mm(da, aup_ref[...]))
    gate = _mm(jax.nn.sigmoid(dg), gup_ref[...])

    seg = seg_ref[...]
    kk = k * kk_ref[...]
    kmod = k * (1.0 + (a - 1.0) * ka_ref[...])
    kk_sq, bonus_dot = _head_sums([kk * kk, r * kmod * rk_ref[...]], seg)
    kk = kk / jnp.maximum(jnp.sqrt(kk_sq), 1e-12)
    bonus = bonus_dot * v

    ri = lax.broadcasted_iota(jnp.int32, (rows, rows), 0)
    ci = lax.broadcasted_iota(jnp.int32, (rows, rows), 1)
    tri = (((ri >> shift) == (ci >> shift)) & (ri >= ci)).astype(BF16)
    w_hi, w_mid, w_lo = _split3(logw)
    cl = _dot(tri, w_hi) + _dot(tri, w_mid) + _dot(tri, w_lo)
    g_incl = jnp.exp(cl)
    g_inv = jnp.exp(-cl)
    a_t = -kk * jnp.exp(cl - logw)
    b_t = kk * a * g_inv
    k_t = kmod * g_inv
    r_t = r * g_incl

    pr = lax.broadcasted_iota(jnp.int32, (gr, pw), 0)
    pc = lax.broadcasted_iota(jnp.int32, (gr, pw), 1) & (RWKV_HEAD - 1)
    p_same = (pr >> shift) == (pc >> shift)
    p_incl = p_same & (pr >= pc)
    p_strict = p_same & (pr > pc)
    p_eye = (pr == pc).astype(F32)
    base_shift = min(shift, int(math.log2(INV_BASE)))
    base_blk = (pr >> base_shift) == (pc >> base_shift)
    level_blks = [((pr >> (lv + 1)) == (pc >> (lv + 1))) & ((pr >> lv) != (pc >> lv))
                  for lv in range(base_shift, shift)]
    bd_mask = ((lax.broadcasted_iota(jnp.int32, (pw, pw), 0) >> 6)
               == (lax.broadcasted_iota(jnp.int32, (pw, pw), 1) >> 6))
    bd_zero = jnp.zeros((pw, pw), BF16)

    def bd(x):
        return jnp.where(bd_mask, jnp.concatenate([x.astype(BF16)] * PACK_HEADS, axis=0), bd_zero)

    def pk(x, y_bd, dims=None):
        return _mm(x, y_bd, dims)

    chains = [(g, p) for g in range(n_grp) for p in range(n_pack)]
    nch = range(len(chains))
    cut = lambda m, g, p: m[g * gr:(g + 1) * gr, p * pw:(p + 1) * pw]
    a_p = [cut(a_t, g, p) for g, p in chains]
    b_p = [cut(b_t, g, p) for g, p in chains]
    k_p = [cut(k_t, g, p) for g, p in chains]
    r_p = [cut(r_t, g, p) for g, p in chains]
    v_p = [cut(v, g, p) for g, p in chains]
    sub = [slice(j * tlen, (j + 1) * tlen) for j in range(spg)]
    ar = [jnp.concatenate([a_p[c], r_p[c]], axis=0) for c in nch]
    b_bd = [bd(b_p[c]) for c in nch]
    k_bd = [bd(k_p[c]) for c in nch]
    v_bd = [bd(v_p[c]) for c in nch]
    gb = [pk(ar[c], b_bd[c], _NT) for c in nch]
    gk = [pk(ar[c], k_bd[c], _NT) for c in nch]
    a_ab = [jnp.where(p_strict, gb[c][:gr], 0.0) for c in nch]
    a_rb = [jnp.where(p_incl, gb[c][gr:], 0.0) for c in nch]
    a_ak = [jnp.where(p_strict, gk[c][:gr], 0.0) for c in nch]
    a_rk = [jnp.where(p_incl, gk[c][gr:], 0.0) for c in nch]
    s_old = [[s_scr[g * spg + j, p] for j in range(spg)] for g, p in chains]
    s_terms = [[pk(jnp.concatenate([a_p[c][sr], r_p[c][sr]], axis=0), s_old[c][j], _NT)
                for j, sr in enumerate(sub)] for c in nch]
    a_s0 = [jnp.concatenate([x[:tlen] for x in s_terms[c]], axis=0) for c in nch]
    r_s0 = [jnp.concatenate([x[tlen:] for x in s_terms[c]], axis=0) for c in nch]
    av = [pk(jnp.concatenate([a_ak[c], a_rk[c]], axis=0), v_bd[c]) for c in nch]
    rhs = [a_s0[c] + av[c][:gr] for c in nch]
    npow = [jnp.where(base_blk, a_ab[c], 0.0) for c in nch]
    inv = [p_eye + npow[c] for c in nch]
    for _ in range(max(0, base_shift - 1)):
        npow = [pk(npow[c], bd(npow[c])) for c in nch]
        inv = [inv[c] + pk(inv[c], bd(npow[c])) for c in nch]
    for off_blk in level_blks:
        cross = [pk(jnp.where(off_blk, a_ab[c], 0.0), bd(inv[c])) for c in nch]
        inv = [inv[c] + pk(inv[c], bd(cross[c])) for c in nch]
    u = [pk(inv[c], bd(rhs[c])) for c in nch]
    outs = [r_s0[c] + pk(a_rb[c], bd(u[c])) + av[c][gr:] for c in nch]
    for c, (g, p) in enumerate(chains):
        for j, sr in enumerate(sub):
            uv = jnp.concatenate([u[c][sr], v_p[c][sr]], axis=0)
            bk = jnp.concatenate([b_p[c][sr], k_p[c][sr]], axis=0)
            last_row = g * gr + sr.stop - 1
            g_last = g_incl[last_row:last_row + 1, p * pw:(p + 1) * pw]
            upd = jnp.where(bd_mask, pk(uv, bk, _TN), 0.0)
            s_scr[g * spg + j, p] = (s_old[c][j] + upd) * g_last
    o = jnp.concatenate(
        [jnp.concatenate([outs[g * n_pack + p] for p in range(n_pack)], axis=1)
         for g in range(n_grp)], axis=0)

    inv_n = 1.0 / RWKV_HEAD
    (o_sum,) = _head_sums([o], seg)
    cen = o - o_sum * inv_n
    (sq_sum,) = _head_sums([cen * cen], seg)
    y = cen * lax.rsqrt(sq_sum * inv_n + GN_EPS) * gng_ref[...] + gnb_ref[...] + bonus
    y = (y * gate).astype(o_ref.dtype)
    if len(o_ref.shape) == 3:
        for i, rs in enumerate(seqs):
            o_ref[i] = y[rs]
    else:
        o_ref[...] = y

    @pl.when(pl.program_id(1) == pl.num_programs(1) - 1)
    def _():
        for q in range(nb):
            for h in range(RWKV_HEADS):
                blk = slice((h % PACK_HEADS) * RWKV_HEAD, (h % PACK_HEADS + 1) * RWKV_HEAD)
                s_out_ref[q, h] = s_scr[q, h // PACK_HEADS, blk, blk]


def _rwkv_group(zr, row0, batch, seq, n_grp, spg, tlen, zp0, s0, p, seg, *, layer, depth, mixed,
                state, name):
    nb = n_grp * spg
    nc = seq // tlen
    assert seq % tlen == 0 and batch % nb == 0 and row0 % tlen == 0 and (spg == 1 or nc == 1)
    const = lambda bi, j: (0, 0)
    vec = lambda n: pl.BlockSpec((1, n), const)
    mat = lambda a: pl.BlockSpec(a.shape, const)
    st = (nb, RWKV_HEADS, RWKV_HEAD, RWKV_HEAD)
    base = row0 // tlen
    if nc > 1:
        zr_specs = [pl.BlockSpec((tlen, RWKV_COLS),
                                 lambda bi, j, i=i: (base + (bi * nb + i) * nc + j, 0))
                    for i in range(nb)]
    else:
        assert row0 % (nb * tlen) == 0
        zr_specs = [pl.BlockSpec((nb * tlen, RWKV_COLS),
                                 lambda bi, j: (row0 // (nb * tlen) + bi, 0))]
    if mixed.ndim == 3:
        o_spec = pl.BlockSpec((nb, tlen, RWKV_DIM), lambda bi, j: (bi, j, 1))
    else:
        assert nc == 1
        o_spec = pl.BlockSpec((nb * tlen, RWKV_DIM), lambda bi, j: (bi, 1))
    n_zr = len(zr_specs)
    return _call_with_carried(
        functools.partial(_rwkv_body, n_grp=n_grp, spg=spg, tlen=tlen, n_zr=n_zr),
        grid=(batch // nb, nc),
        in_specs=zr_specs + [
            pl.BlockSpec((nb, 1, RWKV_COLS), lambda bi, j: (bi, 0, 0)),
            pl.BlockSpec(st, lambda bi, j: (bi, 0, 0, 0)),
            vec(RWKV_COLS), vec(RWKV_DIM), mat(p["w_up"]), vec(RWKV_DIM), mat(p["a_up"]),
            mat(p["g_up"]), vec(RWKV_DIM), vec(RWKV_DIM), vec(RWKV_DIM), vec(RWKV_DIM),
            vec(RWKV_DIM), mat(seg)],
        inputs=[zr] * n_zr + [zp0, s0, p["mu"], p["w0"], p["w_up"], p["a0"], p["a_up"], p["g_up"],
                              p["k_k"], p["k_a"], p["r_k"], p["gn_g"], p["gn_b"], seg],
        out_specs=[o_spec, pl.BlockSpec((None,) + st, lambda bi, j: (layer, bi, 0, 0, 0))],
        out_shape=[jax.ShapeDtypeStruct(mixed.shape, BF16),
                   jax.ShapeDtypeStruct((depth, batch) + st[1:], F32)],
        carried={0: mixed, 1: state},
        scratch_shapes=[pltpu.VMEM((nb, RWKV_HEADS // PACK_HEADS, PACK_HEADS * RWKV_HEAD,
                                    PACK_HEADS * RWKV_HEAD), F32),
                        pltpu.VMEM((nb, 1, RWKV_COLS), F32)],
        semantics=("parallel", "arbitrary"), name=name)


def _attn_body(q_ref, k_ref, v_ref, o_ref, *dense_refs, bb, tq):
    scale = MEM_HEAD_DIM ** -0.5
    pairs = [(i, h) for i in range(bb) for h in range(MEM_HEADS)]
    rows = lambda i: slice(i * tq, (i + 1) * tq)
    cols = lambda h: slice(h * MEM_HEAD_DIM, (h + 1) * MEM_HEAD_DIM)

    if dense_refs:
        for src, dst in zip((k_ref, v_ref), dense_refs):
            for i, h in pairs:
                dst[i, :, cols(h)] = src[i, :, h, :]
        k_ref, v_ref = dense_refs

    def head(ref, i, h):
        return ref[i, :, cols(h)].astype(BF16)

    s = [lax.dot_general(q_ref[rows(i), cols(h)], head(k_ref, i, h), _NT,
                         preferred_element_type=F32) * scale for i, h in pairs]
    e = [jnp.exp(x - jnp.max(x, axis=-1, keepdims=True)) for x in s]
    p = [x / jnp.sum(x, axis=-1, keepdims=True) for x in e]
    o = [_dot(p[n].astype(BF16), head(v_ref, i, h)) for n, (i, h) in enumerate(pairs)]
    for n, (i, h) in enumerate(pairs):
        o_ref[rows(i), cols(h)] = o[n].astype(o_ref.dtype)


def _attn_group(q, row0, batch, seq, tq, bb, mk, mv, *, layer, name):
    nt = seq // tq
    rows = bb * tq
    base = row0 // rows
    assert seq % tq == 0 and batch % bb == 0 and row0 % rows == 0 and (bb == 1 or nt == 1)
    if mk.ndim == 5:
        kv_spec = pl.BlockSpec((None, bb, N_MEM, MEM_HEADS, MEM_HEAD_DIM),
                               lambda bi, j: (layer, bi, 0, 0, 0))
        scratch = [pltpu.VMEM((bb, N_MEM, D_MODEL), F32)] * 2
    else:
        kv_spec = pl.BlockSpec((bb, N_MEM, D_MODEL), lambda bi, j: (bi, 0, 0))
        scratch = []
    return pl.pallas_call(
        functools.partial(_attn_body, bb=bb, tq=tq),
        grid=(batch // bb, nt),
        in_specs=[pl.BlockSpec((rows, D_MODEL), lambda bi, j: (base + bi * nt + j, 0)),
                  kv_spec, kv_spec],
        out_specs=pl.BlockSpec((rows, D_MODEL), lambda bi, j: (bi * nt + j, 0)),
        out_shape=jax.ShapeDtypeStruct((batch * seq, D_MODEL), BF16),
        scratch_shapes=scratch,
        compiler_params=_params(("parallel", "arbitrary")), name=name,
    )(q, mk, mv)


def kernel(x_prompt, x_sample, cache_mem_k, cache_mem_v, state_wkv, state_conv, state_shift,
           mem_prompt, w_in, mu_shift, conv_w, conv_b, conv_ln_g, conv_ln_b, rwkv_w0, rwkv_w_up,
           rwkv_a0, rwkv_a_up, rwkv_g_up, rwkv_k_k, rwkv_k_a, rwkv_r_k, rwkv_gn_g, rwkv_gn_b,
           w_out, mem_norm_g, w_q, w_k, w_v, w_o, w_ffn1, w_ffn2, norm_g):
    bp, tp, _ = x_prompt.shape
    bs, ts, _ = x_sample.shape
    depth = w_in.shape[0]
    n_p, n_s = bp * tp, bs * ts
    row = lambda a: a.reshape(1, -1)
    sample_nb = SAMPLE_STEP_ROWS // ts
    prompt_nb = min(RWKV_PROMPT_SEQS, bp)

    x = jnp.concatenate([x_prompt.reshape(n_p, D_MODEL), x_sample.reshape(n_s, D_MODEL)], axis=0)
    mem2d = mem_prompt.reshape(bp * N_MEM, D_MODEL)
    lane_head = jnp.arange(V7X_LANES) // RWKV_HEAD
    seg = (lane_head[:, None] == lane_head[None, :]).astype(BF16)
    zero_conv = jnp.zeros((bp, CONV_BUF, CONV_CH), F32)
    zero_prev = jnp.zeros((bp, 1, RWKV_COLS), F32)
    zero_wkv = jnp.zeros((bp, RWKV_HEADS, RWKV_HEAD, RWKV_HEAD), F32)

    in_splits = [(0, 2 * CONV_CH), (2 * CONV_CH, IN_COLS)]
    w_in_layers = [w_in[l].astype(BF16) for l in range(depth)]
    zc, zr = _norm_matmul(x, norm_g[0][0:1], w_in_layers[0], in_splits, out_dtype=F32, name="in_proj")

    shift_p, shift_s = [], []
    mk5 = mv5 = wkv_p = conv_p = wkv_s = conv_s = None
    for l in range(depth):
        g = norm_g[l]
        w_in_b = w_in_layers[l]
        rp = dict(mu=row(mu_shift[l]), w0=row(rwkv_w0[l]), w_up=rwkv_w_up[l].astype(BF16),
                  a0=row(rwkv_a0[l]), a_up=rwkv_a_up[l].astype(BF16),
                  g_up=rwkv_g_up[l].astype(BF16), k_k=row(rwkv_k_k[l]), k_a=row(rwkv_k_a[l]),
                  r_k=row(rwkv_r_k[l]), gn_g=row(rwkv_gn_g[l]), gn_b=row(rwkv_gn_b[l]))
        cw, cb = conv_w[l], row(conv_b[l])
        clg, clb = row(conv_ln_g[l]), row(conv_ln_b[l])

        sub = V7X_SUBLANES
        tail_p = _rms_rows(x, g[0:1], tp - sub, tp, bp, sub, name="shift_prompt")
        shift_p.append(tail_p.reshape(bp, sub, D_MODEL)[:, sub - 1])
        tail_s = _rms_rows(x, g[0:1], n_p, n_s, 1, n_s, name="shift_sample")
        shift_s.append(tail_s.reshape(bs, ts, D_MODEL)[:, ts - 1])

        (zp_s,) = _norm_matmul(state_shift[l], g[0:1], w_in_b, in_splits[1:], out_dtype=F32,
                               normalize=False, name="prev_proj")

        mixed_p, conv_p = _conv_group(zc, 0, bp, tp, 1, min(CONV_TILE, tp), zero_conv, cw, cb,
                                      clg, clb, layer=l, depth=depth, state=conv_p,
                                      seq_major=True, name="conv_prompt")
        mixed_s, conv_s = _conv_group(zc, n_p, bs, ts, sample_nb, ts, state_conv[l], cw, cb, clg,
                                      clb, layer=l, depth=depth, state=conv_s, seq_major=False,
                                      name="conv_sample")
        mixed_p, wkv_p = _rwkv_group(zr, 0, bp, tp, prompt_nb, 1, RWKV_GROUP_ROWS, zero_prev, zero_wkv,
                                     rp, seg, layer=l, depth=depth, mixed=mixed_p, state=wkv_p,
                                     name="rwkv_prompt")
        mixed_s, wkv_s = _rwkv_group(zr, n_p, bs, ts, 1, RWKV_GROUP_ROWS // ts, ts, zp_s[:, None, :], state_wkv[l],
                                     rp, seg, layer=l, depth=depth, mixed=mixed_s, state=wkv_s,
                                     name="rwkv_sample")

        x, q = _res_proj([mixed_p.reshape(n_p, D_MODEL), mixed_s], w_out[l].astype(BF16), x, g[1:2],
                         g_b=g[2:3], w_b=w_q[l].astype(BF16), splits=[(0, D_MODEL)], out_dtype=BF16,
                         name="out_q_proj")
        mk2, mv2, mk5, mv5 = _mem_kv_proj(mem2d, row(mem_norm_g), w_k[l].astype(BF16),
                                          w_v[l].astype(BF16), layer=l, depth=depth, k5=mk5, v5=mv5)
        att_p = _attn_group(q, 0, bp, tp, min(ATTN_TILE, tp), 1, mk2.reshape(bp, N_MEM, D_MODEL),
                            mv2.reshape(bp, N_MEM, D_MODEL), layer=l, name="attn_prompt")
        att_s = _attn_group(q, n_p, bs, ts, ts, ATTN_SAMPLE_BATCH, cache_mem_k, cache_mem_v,
                            layer=l, name="attn_sample")
        x, f = _res_proj([att_p, att_s], w_o[l].astype(BF16), x, g[3:4], g_b=g[4:5],
                         w_b=w_ffn1[l].astype(BF16), splits=[(0, D_FF)], relu2=True, out_dtype=BF16,
                         name="attn_out_ffn_up")
        w_down = w_ffn2[l].astype(BF16)
        if l + 1 < depth:
            x, zc, zr = _res_proj([f], w_down, x, g[5:6], g_b=norm_g[l + 1][0:1],
                                  w_b=w_in_layers[l + 1], splits=in_splits, out_dtype=F32,
                                  name="ffn_down_in_proj")
        else:
            (y_p,) = _res_proj([f], w_down, x, g[5:6], row0=0, n_rows=n_p, name="ffn_down_prompt")
            (y_s,) = _res_proj([f], w_down, x, g[5:6], row0=n_p, n_rows=n_s, name="ffn_down_sample")

    return (y_p.reshape(bp, tp, D_MODEL), y_s.reshape(bs, ts, D_MODEL), mk5, mv5, wkv_p, conv_p,
            jnp.stack(shift_p), wkv_s, conv_s, jnp.stack(shift_s))
```

```python
import functools
import math

import jax
import jax.numpy as jnp
from jax import lax
from jax.experimental import pallas as pl
from jax.experimental.pallas import tpu as pltpu

F32 = jnp.float32
BF16 = jnp.bfloat16

D_MODEL = 1024
CONV_CH = 512
RWKV_DIM = 512
RWKV_HEAD = 64
RWKV_HEADS = 8
CONV_WIDTH = 31
CONV_BUF = CONV_WIDTH - 1
DECAY_RANK = 64
ICLR_RANK = 64
GATE_RANK = 128
RWKV_COLS = 3 * RWKV_DIM + DECAY_RANK + ICLR_RANK + GATE_RANK
IN_COLS = 2 * CONV_CH + RWKV_COLS
N_MEM = 256
MEM_HEADS = 4
MEM_HEAD_DIM = D_MODEL // MEM_HEADS
D_FF = 4 * D_MODEL
RMS_EPS = 1e-6
LN_EPS = 1e-5
GN_EPS = 64e-5

V7X_LANES = 128
V7X_SUBLANES = 8
V7X_VMEM_LIMIT_BYTES = 56 * 1024 * 1024

ROW_BLOCK = 512
COL_CHUNK = 1024
RWKV_GROUP_ROWS = 64
RWKV_PROMPT_SEQS = 4
PACK_HEADS = 4
SAMPLE_STEP_ROWS = 64
INV_BASE = 8
ATTN_TILE = 512
ATTN_SAMPLE_BATCH = 4

_NT = (((1,), (1,)), ((), ()))
_TN = (((0,), (0,)), ((), ()))


def _params(semantics):
    return pltpu.CompilerParams(dimension_semantics=semantics,
                                vmem_limit_bytes=V7X_VMEM_LIMIT_BYTES)


def _dot(a, b):
    return jnp.dot(a, b, preferred_element_type=F32)


def _mm(a, b, dims=None):
    if dims is None:
        dims = (((a.ndim - 1,), (0,)), ((), ()))
    return lax.dot_general(a.astype(BF16), b.astype(BF16), dims, preferred_element_type=F32)


def _split2(x):
    hi = x.astype(BF16)
    lo = (x - hi.astype(F32)).astype(BF16)
    return hi, lo


def _split3(x):
    hi = x.astype(BF16)
    r1 = x - hi.astype(F32)
    mid = r1.astype(BF16)
    lo = (r1 - mid.astype(F32)).astype(BF16)
    return hi, mid, lo


def _rms(x, g):
    ms = jnp.mean(x * x, axis=-1, keepdims=True)
    return x * lax.rsqrt(ms + RMS_EPS) * g


def _drop_carried(body, n_in, n_carried):
    def wrapped(*refs):
        return body(*refs[:n_in], *refs[n_in + n_carried:])
    return wrapped


def _call_with_carried(body, *, grid, in_specs, inputs, out_specs, out_shape, carried,
                       scratch_shapes=(), semantics, name):
    n_in = len(inputs)
    extra, aliases = [], {}
    for out_idx, buf in sorted(carried.items()):
        if buf is None:
            continue
        aliases[n_in + len(extra)] = out_idx
        extra.append(buf)
    return pl.pallas_call(
        _drop_carried(body, n_in, len(extra)),
        grid=grid,
        in_specs=list(in_specs) + [pl.BlockSpec(memory_space=pl.ANY)] * len(extra),
        out_specs=out_specs, out_shape=out_shape, scratch_shapes=list(scratch_shapes),
        input_output_aliases=aliases, compiler_params=_params(semantics), name=name,
    )(*inputs, *extra)


def _resident(shape):
    return pl.BlockSpec(shape, lambda i: (0,) * len(shape), pipeline_mode=pl.Buffered(1))


def _project(hb, w_ref, o_refs, splits, relu2):
    for (lo, hi), o_ref in zip(splits, o_refs):
        for c0 in range(lo, hi, COL_CHUNK):
            c1 = min(hi, c0 + COL_CHUNK)
            y = _dot(hb, w_ref[:, c0:c1])
            if relu2:
                y = jnp.square(jnp.maximum(y, 0.0))
            o_ref[:, c0 - lo:c1 - lo] = y.astype(o_ref.dtype)


def _norm_matmul_body(x_ref, g_ref, w_ref, *o_refs, splits, normalize, relu2):
    x = x_ref[...]
    h = _rms(x, g_ref[...]) if normalize else x
    _project(h.astype(BF16), w_ref, o_refs, splits, relu2)


def _norm_matmul(x, g, w, splits, *, out_dtype, normalize=True, relu2=False, row0=0, n_rows=None,
                 name):
    k = x.shape[1]
    m = x.shape[0] if n_rows is None else n_rows
    bm = math.gcd(ROW_BLOCK, m, row0)
    return pl.pallas_call(
        functools.partial(_norm_matmul_body, splits=tuple(splits), normalize=normalize,
                          relu2=relu2),
        grid=(m // bm,),
        in_specs=[pl.BlockSpec((bm, k), lambda i: (row0 // bm + i, 0)), _resident((1, k)),
                  _resident(w.shape)],
        out_specs=[pl.BlockSpec((bm, hi - lo), lambda i: (i, 0)) for lo, hi in splits],
        out_shape=[jax.ShapeDtypeStruct((m, hi - lo), out_dtype) for lo, hi in splits],
        compiler_params=_params(("parallel",)), name=name,
    )(x, g, w)


def _res_proj_body(*refs, n_lhs, first_blocks, splits, relu2):
    y_refs = refs[:n_lhs]
    w_a_ref, x_ref, g_a_ref = refs[n_lhs:n_lhs + 3]
    rest = refs[n_lhs + 3:]
    if splits:
        g_b_ref, w_b_ref = rest[:2]
        rest = rest[2:]
    o_ref, p_refs = rest[0], rest[1:]

    def emit(y_ref):
        x_new = x_ref[...] + _rms(_dot(y_ref[...], w_a_ref[...]), g_a_ref[...])
        o_ref[...] = x_new
        if splits:
            _project(_rms(x_new, g_b_ref[...]).astype(BF16), w_b_ref, p_refs, splits, relu2)

    if n_lhs == 1:
        emit(y_refs[0])
    else:
        i = pl.program_id(0)
        pl.when(i < first_blocks)(lambda: emit(y_refs[0]))
        pl.when(i >= first_blocks)(lambda: emit(y_refs[1]))


def _res_proj(lhs, w_a, x, g_a, *, g_b=None, w_b=None, splits=(), relu2=False, out_dtype=None,
              row0=0, n_rows=None, name):
    m, n = x.shape
    k = w_a.shape[0]
    n_rows = m if n_rows is None else n_rows
    bm = math.gcd(ROW_BLOCK, n_rows, row0, *[a.shape[0] for a in lhs])
    if len(lhs) == 1:
        lhs_specs = [pl.BlockSpec((bm, k), lambda i: (row0 // bm + i, 0))]
        first_blocks = None
    else:
        assert row0 == 0 and n_rows == m and sum(a.shape[0] for a in lhs) == m
        first_blocks = lhs[0].shape[0] // bm
        lhs_specs = [pl.BlockSpec((bm, k), lambda i: (jnp.minimum(i, first_blocks - 1), 0)),
                     pl.BlockSpec((bm, k), lambda i: (jnp.maximum(i - first_blocks, 0), 0))]
    splits = tuple(splits)
    row_out = lambda width: pl.BlockSpec((bm, width), lambda i: (i, 0))
    proj_in = [_resident((1, n)), _resident(w_b.shape)] if splits else []
    proj_args = [g_b, w_b] if splits else []
    return pl.pallas_call(
        functools.partial(_res_proj_body, n_lhs=len(lhs), first_blocks=first_blocks,
                          splits=splits, relu2=relu2),
        grid=(n_rows // bm,),
        in_specs=lhs_specs + [_resident((k, n)),
                              pl.BlockSpec((bm, n), lambda i: (row0 // bm + i, 0)),
                              _resident((1, n))] + proj_in,
        out_specs=[row_out(n)] + [row_out(hi - lo) for lo, hi in splits],
        out_shape=[jax.ShapeDtypeStruct((n_rows, n), F32)]
        + [jax.ShapeDtypeStruct((n_rows, hi - lo), out_dtype) for lo, hi in splits],
        compiler_params=_params(("parallel",)), name=name,
    )(*lhs, w_a, x, g_a, *proj_args)


def _rms_rows_body(x_ref, g_ref, o_ref):
    o_ref[...] = _rms(x_ref[...], g_ref[...])


def _rms_rows(x, g, row0, stride, n_blocks, bm, *, name):
    k = x.shape[1]
    assert row0 % bm == 0 and stride % bm == 0
    return pl.pallas_call(
        _rms_rows_body, grid=(n_blocks,),
        in_specs=[pl.BlockSpec((bm, k), lambda i: (row0 // bm + i * (stride // bm), 0)),
                  _resident((1, k))],
        out_specs=pl.BlockSpec((bm, k), lambda i: (i, 0)),
        out_shape=jax.ShapeDtypeStruct((n_blocks * bm, k), F32),
        compiler_params=_params(("parallel",)), name=name,
    )(x, g)


def _mem_kv_body(x_ref, g_ref, wk_ref, wv_ref, k2_ref, v2_ref, k5_ref, v5_ref, *, nb):
    hb = _rms(x_ref[...], g_ref[...]).astype(BF16)
    for w_ref, o2_ref, o5_ref in ((wk_ref, k2_ref, k5_ref), (wv_ref, v2_ref, v5_ref)):
        y = _dot(hb, w_ref[...])
        o2_ref[...] = y.astype(o2_ref.dtype)
        for i in range(nb):
            for h in range(MEM_HEADS):
                o5_ref[i, :, h, :] = y[i * N_MEM:(i + 1) * N_MEM,
                                       h * MEM_HEAD_DIM:(h + 1) * MEM_HEAD_DIM]


def _mem_kv_proj(mem2d, g, wk, wv, *, layer, depth, k5, v5):
    m, k = mem2d.shape
    nb = ROW_BLOCK // N_MEM
    batch = m // N_MEM
    assert batch % nb == 0
    row_spec = pl.BlockSpec((nb * N_MEM, D_MODEL), lambda i: (i, 0))
    w_spec = pl.BlockSpec((k, D_MODEL), lambda i: (0, 0))
    out5 = pl.BlockSpec((None, nb, N_MEM, MEM_HEADS, MEM_HEAD_DIM), lambda i: (layer, i, 0, 0, 0))
    shape5 = jax.ShapeDtypeStruct((depth, batch, N_MEM, MEM_HEADS, MEM_HEAD_DIM), F32)
    return _call_with_carried(
        functools.partial(_mem_kv_body, nb=nb),
        grid=(batch // nb,),
        in_specs=[row_spec, pl.BlockSpec((1, k), lambda i: (0, 0)), w_spec, w_spec],
        inputs=[mem2d, g, wk, wv],
        out_specs=[row_spec, row_spec, out5, out5],
        out_shape=[jax.ShapeDtypeStruct((m, D_MODEL), BF16), jax.ShapeDtypeStruct((m, D_MODEL), BF16),
                   shape5, shape5],
        carried={2: k5, 3: v5}, semantics=("parallel",), name="mem_kv_proj")


CONV_CUR_ROW = CONV_BUF + (-CONV_BUF) % V7X_SUBLANES
CONV_HALO_ROW = CONV_CUR_ROW - CONV_BUF


def _conv_tile(z, ext_ref, sh_ref, w_ref, b_ref, lg_ref, lb_ref):
    tt = z.shape[0]
    h0, cur = CONV_HALO_ROW, CONV_CUR_ROW
    ext_ref[cur:cur + tt, :] = z[:, :CONV_CH] * jax.nn.sigmoid(z[:, CONV_CH:])
    acc = None
    for r in range(V7X_SUBLANES):
        taps = range(r, CONV_WIDTH, V7X_SUBLANES)
        span = tt + taps[-1] - r
        sh_ref[r, 0:span, :] = ext_ref[h0 + r:h0 + r + span, :]
        for k in taps:
            term = sh_ref[r, k - r:k - r + tt, :] * w_ref[k:k + 1, :]
            acc = term if acc is None else acc + term
    y = acc + b_ref[...]
    mu = jnp.mean(y, axis=-1, keepdims=True)
    var = jnp.mean(jnp.square(y - mu), axis=-1, keepdims=True)
    yn = (y - mu) * lax.rsqrt(var + LN_EPS) * lg_ref[...] + lb_ref[...]
    return yn * jax.nn.sigmoid(yn), ext_ref[tt + h0:tt + cur, :]


def _conv_scratch(tt):
    return [pltpu.VMEM((CONV_CUR_ROW + tt, CONV_CH), F32),
            pltpu.VMEM((V7X_SUBLANES, tt + CONV_BUF, CONV_CH), F32)]


def _conv_body(z_ref, buf_ref, w_ref, b_ref, lg_ref, lb_ref, y_ref, nc_ref, ext_ref, sh_ref,
               *, nb, tt):
    for i in range(nb):
        rows = slice(i * tt, (i + 1) * tt)
        ext_ref[CONV_HALO_ROW:CONV_CUR_ROW, :] = buf_ref[i]
        out, halo = _conv_tile(z_ref[rows, :], ext_ref, sh_ref, w_ref, b_ref, lg_ref, lb_ref)
        y_ref[rows, :] = out.astype(y_ref.dtype)
        nc_ref[i] = halo


def _conv_group(zc, batch, seq, nb, conv_buf, w, b, lg, lb, *, layer, depth, state, name):
    rows = nb * seq
    assert batch % nb == 0 and zc.shape[0] == batch * seq
    const = lambda bi: (0, 0)
    return _call_with_carried(
        functools.partial(_conv_body, nb=nb, tt=seq),
        grid=(batch // nb,),
        in_specs=[pl.BlockSpec((rows, 2 * CONV_CH), lambda bi: (bi, 0)),
                  pl.BlockSpec((nb, CONV_BUF, CONV_CH), lambda bi: (bi, 0, 0)),
                  pl.BlockSpec((CONV_WIDTH, CONV_CH), const),
                  pl.BlockSpec((1, CONV_CH), const), pl.BlockSpec((1, CONV_CH), const),
                  pl.BlockSpec((1, CONV_CH), const)],
        inputs=[zc, conv_buf, w, b, lg, lb],
        out_specs=[pl.BlockSpec((rows, CONV_CH), lambda bi: (bi, 0)),
                   pl.BlockSpec((None, nb, CONV_BUF, CONV_CH), lambda bi: (layer, bi, 0, 0))],
        out_shape=[jax.ShapeDtypeStruct((batch * seq, CONV_CH + RWKV_DIM), BF16),
                   jax.ShapeDtypeStruct((depth, batch, CONV_BUF, CONV_CH), F32)],
        carried={1: state}, scratch_shapes=_conv_scratch(seq),
        semantics=("parallel",), name=name)


def _in_proj_conv_body(x_ref, g_ref, w_ref, buf_ref, cw_ref, cb_ref, lg_ref, lb_ref,
                       zr_ref, y_ref, nc_ref, ext_ref, sh_ref, *, tiles_per_seq):
    @pl.when(pl.program_id(0) % tiles_per_seq == 0)
    def _():
        ext_ref[CONV_HALO_ROW:CONV_CUR_ROW, :] = buf_ref[0]

    hb = _rms(x_ref[...], g_ref[...]).astype(BF16)
    z = _dot(hb, w_ref[:, 0:2 * CONV_CH])
    out, halo = _conv_tile(z, ext_ref, sh_ref, cw_ref, cb_ref, lg_ref, lb_ref)
    y_ref[...] = out.astype(y_ref.dtype)
    ext_ref[CONV_HALO_ROW:CONV_CUR_ROW, :] = halo
    nc_ref[0] = halo
    _project(hb, w_ref, [zr_ref], [(2 * CONV_CH, IN_COLS)], False)


def _in_proj_conv(x, g, w, batch, seq, conv_buf, cw, cb, lg, lb, *, layer, depth, state):
    k = x.shape[1]
    tt = math.gcd(ROW_BLOCK, seq)
    tiles = seq // tt
    const = lambda i: (0, 0)
    return _call_with_carried(
        functools.partial(_in_proj_conv_body, tiles_per_seq=tiles),
        grid=(batch * tiles,),
        in_specs=[pl.BlockSpec((tt, k), lambda i: (i, 0)), _resident((1, k)), _resident(w.shape),
                  pl.BlockSpec((1, CONV_BUF, CONV_CH), lambda i: (i // tiles, 0, 0)),
                  pl.BlockSpec((CONV_WIDTH, CONV_CH), const),
                  pl.BlockSpec((1, CONV_CH), const), pl.BlockSpec((1, CONV_CH), const),
                  pl.BlockSpec((1, CONV_CH), const)],
        inputs=[x, g, w, conv_buf, cw, cb, lg, lb],
        out_specs=[pl.BlockSpec((tt, RWKV_COLS), lambda i: (i, 0)),
                   pl.BlockSpec((None, tt, CONV_CH), lambda i: (i // tiles, i % tiles, 0)),
                   pl.BlockSpec((None, 1, CONV_BUF, CONV_CH), lambda i: (layer, i // tiles, 0, 0))],
        out_shape=[jax.ShapeDtypeStruct((batch * seq, RWKV_COLS), F32),
                   jax.ShapeDtypeStruct((batch, seq, CONV_CH + RWKV_DIM), BF16),
                   jax.ShapeDtypeStruct((depth, batch, CONV_BUF, CONV_CH), F32)],
        carried={2: state}, scratch_shapes=_conv_scratch(tt),
        semantics=("arbitrary",), name="in_proj_conv")


def _head_sums(xs, seg):
    rows = xs[0].shape[0]
    groups = RWKV_DIM // V7X_LANES
    stacked = jnp.concatenate(
        [x[:, g * V7X_LANES:(g + 1) * V7X_LANES] for x in xs for g in range(groups)], axis=0)
    hi, lo = _split2(stacked)
    y = _dot(hi, seg) + _dot(lo, seg)
    outs = []
    for i in range(len(xs)):
        parts = [y[(i * groups + g) * rows:(i * groups + g + 1) * rows] for g in range(groups)]
        outs.append(jnp.concatenate(parts, axis=1))
    return outs


def _block_diag(blocks):
    n = len(blocks)
    w = blocks[0].shape[1]
    rows = []
    for i, blk in enumerate(blocks):
        pieces = ([jnp.zeros((blk.shape[0], i * w), blk.dtype)] if i else []) + [blk]
        if i < n - 1:
            pieces.append(jnp.zeros((blk.shape[0], (n - 1 - i) * w), blk.dtype))
        rows.append(jnp.concatenate(pieces, axis=1))
    return jnp.concatenate(rows, axis=0)


def _rwkv_body(*refs, n_grp, spg, tlen, n_zr):
    zr_refs = refs[:n_zr]
    (zp0_ref, s0_ref, mu_ref, w0_ref, wup_ref, a0_ref, aup_ref, gup_ref, kk_ref, ka_ref, rk_ref,
     gng_ref, gnb_ref, seg_ref, o_ref, s_out_ref, s_scr, carry_scr) = refs[n_zr:]
    nb = n_grp * spg
    rows = nb * tlen
    gr = spg * tlen
    shift = int(math.log2(tlen))
    n_pack = RWKV_HEADS // PACK_HEADS
    pw = PACK_HEADS * RWKV_HEAD
    assert 1 << shift == tlen and gr == RWKV_HEAD == RWKV_GROUP_ROWS

    @pl.when(pl.program_id(1) == 0)
    def _():
        carry_scr[...] = zp0_ref[...]
        for q in range(nb):
            for p in range(n_pack):
                s_scr[q, p] = _block_diag([s0_ref[q, p * PACK_HEADS + h] for h in range(PACK_HEADS)])

    if n_zr == 1:
        zr = zr_refs[0][...]
    else:
        zr = jnp.concatenate([r[...] for r in zr_refs], axis=0)
    seqs = [slice(i * tlen, (i + 1) * tlen) for i in range(nb)]
    t_idx = lax.broadcasted_iota(jnp.int32, (rows, 1), 0) & (tlen - 1)
    if nb == 1:
        carry_rows = carry_scr[0]
    else:
        carry_rows = jnp.concatenate(
            [jnp.broadcast_to(carry_scr[i], (tlen, RWKV_COLS)) for i in range(nb)], axis=0)
    prev = jnp.where(t_idx == 0, carry_rows, pltpu.roll(zr, 1, axis=0))
    for i, rs in enumerate(seqs):
        carry_scr[i] = zr[rs.stop - 1:rs.stop, :]
    zs = zr + (prev - zr) * mu_ref[...]

    d = RWKV_DIM
    r = zs[:, 0:d]
    k = zs[:, d:2 * d]
    v = zs[:, 2 * d:3 * d]
    dw = zs[:, 3 * d:3 * d + DECAY_RANK]
    da = zs[:, 3 * d + DECAY_RANK:3 * d + DECAY_RANK + ICLR_RANK]
    dg = zs[:, 3 * d + DECAY_RANK + ICLR_RANK:]

    lw = w0_ref[...] + _mm(jnp.tanh(dw), wup_ref[...])
    logw = -math.exp(-0.5) * jax.nn.sigmoid(lw)
    a = jax.nn.sigmoid(a0_ref[...] + _mm(da, aup_ref[...]))
    gate = _mm(jax.nn.sigmoid(dg), gup_ref[...])

    seg = seg_ref[...]
    kk = k * kk_ref[...]
    kmod = k * (1.0 + (a - 1.0) * ka_ref[...])
    kk_sq, bonus_dot = _head_sums([kk * kk, r * kmod * rk_ref[...]], seg)
    kk = kk / jnp.maximum(jnp.sqrt(kk_sq), 1e-12)
    bonus = bonus_dot * v

    ri = lax.broadcasted_iota(jnp.int32, (rows, rows), 0)
    ci = lax.broadcasted_iota(jnp.int32, (rows, rows), 1)
    tri = (((ri >> shift) == (ci >> shift)) & (ri >= ci)).astype(BF16)
    w_hi, w_mid, w_lo = _split3(logw)
    cl = _dot(tri, w_hi) + _dot(tri, w_mid) + _dot(tri, w_lo)
    g_incl = jnp.exp(cl)
    g_inv = jnp.exp(-cl)
    a_t = -kk * jnp.exp(cl - logw)
    b_t = kk * a * g_inv
    k_t = kmod * g_inv
    r_t = r * g_incl

    pr = lax.broadcasted_iota(jnp.int32, (gr, pw), 0)
    pc = lax.broadcasted_iota(jnp.int32, (gr, pw), 1) & (RWKV_HEAD - 1)
    p_same = (pr >> shift) == (pc >> shift)
    p_incl = p_same & (pr >= pc)
    p_strict = p_same & (pr > pc)
    p_eye = (pr == pc).astype(F32)
    base_shift = min(shift, int(math.log2(INV_BASE)))
    base_blk = (pr >> base_shift) == (pc >> base_shift)
    level_blks = [((pr >> (lv + 1)) == (pc >> (lv + 1))) & ((pr >> lv) != (pc >> lv))
                  for lv in range(base_shift, shift)]
    bd_mask = ((lax.broadcasted_iota(jnp.int32, (pw, pw), 0) >> 6)
               == (lax.broadcasted_iota(jnp.int32, (pw, pw), 1) >> 6))
    bd_zero = jnp.zeros((pw, pw), BF16)

    def bd(x):
        return jnp.where(bd_mask, jnp.concatenate([x.astype(BF16)] * PACK_HEADS, axis=0), bd_zero)

    def pk(x, y_bd, dims=None):
        return _mm(x, y_bd, dims)

    chains = [(g, p) for g in range(n_grp) for p in range(n_pack)]
    nch = range(len(chains))
    cut = lambda m, g, p: m[g * gr:(g + 1) * gr, p * pw:(p + 1) * pw]
    a_p = [cut(a_t, g, p) for g, p in chains]
    b_p = [cut(b_t, g, p) for g, p in chains]
    k_p = [cut(k_t, g, p) for g, p in chains]
    r_p = [cut(r_t, g, p) for g, p in chains]
    v_p = [cut(v, g, p) for g, p in chains]
    sub = [slice(j * tlen, (j + 1) * tlen) for j in range(spg)]
    ar = [jnp.concatenate([a_p[c], r_p[c]], axis=0) for c in nch]
    b_bd = [bd(b_p[c]) for c in nch]
    k_bd = [bd(k_p[c]) for c in nch]
    v_bd = [bd(v_p[c]) for c in nch]
    gb = [pk(ar[c], b_bd[c], _NT) for c in nch]
    gk = [pk(ar[c], k_bd[c], _NT) for c in nch]
    a_ab = [jnp.where(p_strict, gb[c][:gr], 0.0) for c in nch]
    a_rb = [jnp.where(p_incl, gb[c][gr:], 0.0) for c in nch]
    a_ak = [jnp.where(p_strict, gk[c][:gr], 0.0) for c in nch]
    a_rk = [jnp.where(p_incl, gk[c][gr:], 0.0) for c in nch]
    s_old = [[s_scr[g * spg + j, p] for j in range(spg)] for g, p in chains]
    s_terms = [[pk(jnp.concatenate([a_p[c][sr], r_p[c][sr]], axis=0), s_old[c][j], _NT)
                for j, sr in enumerate(sub)] for c in nch]
    a_s0 = [jnp.concatenate([x[:tlen] for x in s_terms[c]], axis=0) for c in nch]
    r_s0 = [jnp.concatenate([x[tlen:] for x in s_terms[c]], axis=0) for c in nch]
    av = [pk(jnp.concatenate([a_ak[c], a_rk[c]], axis=0), v_bd[c]) for c in nch]
    rhs = [a_s0[c] + av[c][:gr] for c in nch]
    npow = [jnp.where(base_blk, a_ab[c], 0.0) for c in nch]
    inv = [p_eye + npow[c] for c in nch]
    for _ in range(max(0, base_shift - 1)):
        npow = [pk(npow[c], bd(npow[c])) for c in nch]
        inv = [inv[c] + pk(inv[c], bd(npow[c])) for c in nch]
    for off_blk in level_blks:
        cross = [pk(jnp.where(off_blk, a_ab[c], 0.0), bd(inv[c])) for c in nch]
        inv = [inv[c] + pk(inv[c], bd(cross[c])) for c in nch]
    u = [pk(inv[c], bd(rhs[c])) for c in nch]
    outs = [r_s0[c] + pk(a_rb[c], bd(u[c])) + av[c][gr:] for c in nch]
    for c, (g, p) in enumerate(chains):
        for j, sr in enumerate(sub):
            uv = jnp.concatenate([u[c][sr], v_p[c][sr]], axis=0)
            bk = jnp.concatenate([b_p[c][sr], k_p[c][sr]], axis=0)
            last_row = g * gr + sr.stop - 1
            g_last = g_incl[last_row:last_row + 1, p * pw:(p + 1) * pw]
            upd = jnp.where(bd_mask, pk(uv, bk, _TN), 0.0)
            s_scr[g * spg + j, p] = (s_old[c][j] + upd) * g_last
    o = jnp.concatenate(
        [jnp.concatenate([outs[g * n_pack + p] for p in range(n_pack)], axis=1)
         for g in range(n_grp)], axis=0)

    inv_n = 1.0 / RWKV_HEAD
    (o_sum,) = _head_sums([o], seg)
    cen = o - o_sum * inv_n
    (sq_sum,) = _head_sums([cen * cen], seg)
    y = cen * lax.rsqrt(sq_sum * inv_n + GN_EPS) * gng_ref[...] + gnb_ref[...] + bonus
    y = (y * gate).astype(o_ref.dtype)
    if len(o_ref.shape) == 3:
        for i, rs in enumerate(seqs):
            o_ref[i] = y[rs]
    else:
        o_ref[...] = y

    @pl.when(pl.program_id(1) == pl.num_programs(1) - 1)
    def _():
        for q in range(nb):
            for h in range(RWKV_HEADS):
                blk = slice((h % PACK_HEADS) * RWKV_HEAD, (h % PACK_HEADS + 1) * RWKV_HEAD)
                s_out_ref[q, h] = s_scr[q, h // PACK_HEADS, blk, blk]


def _rwkv_group(zr, row0, batch, seq, n_grp, spg, tlen, zp0, s0, p, seg, *, layer, depth, mixed,
                state, name):
    nb = n_grp * spg
    nc = seq // tlen
    assert seq % tlen == 0 and batch % nb == 0 and row0 % tlen == 0 and (spg == 1 or nc == 1)
    const = lambda bi, j: (0, 0)
    vec = lambda n: pl.BlockSpec((1, n), const)
    mat = lambda a: pl.BlockSpec(a.shape, const)
    st = (nb, RWKV_HEADS, RWKV_HEAD, RWKV_HEAD)
    base = row0 // tlen
    if nc > 1:
        zr_specs = [pl.BlockSpec((tlen, RWKV_COLS),
                                 lambda bi, j, i=i: (base + (bi * nb + i) * nc + j, 0))
                    for i in range(nb)]
    else:
        assert row0 % (nb * tlen) == 0
        zr_specs = [pl.BlockSpec((nb * tlen, RWKV_COLS),
                                 lambda bi, j: (row0 // (nb * tlen) + bi, 0))]
    if mixed.ndim == 3:
        o_spec = pl.BlockSpec((nb, tlen, RWKV_DIM), lambda bi, j: (bi, j, 1))
    else:
        assert nc == 1
        o_spec = pl.BlockSpec((nb * tlen, RWKV_DIM), lambda bi, j: (bi, 1))
    n_zr = len(zr_specs)
    return _call_with_carried(
        functools.partial(_rwkv_body, n_grp=n_grp, spg=spg, tlen=tlen, n_zr=n_zr),
        grid=(batch // nb, nc),
        in_specs=zr_specs + [
            pl.BlockSpec((nb, 1, RWKV_COLS), lambda bi, j: (bi, 0, 0)),
            pl.BlockSpec(st, lambda bi, j: (bi, 0, 0, 0)),
            vec(RWKV_COLS), vec(RWKV_DIM), mat(p["w_up"]), vec(RWKV_DIM), mat(p["a_up"]),
            mat(p["g_up"]), vec(RWKV_DIM), vec(RWKV_DIM), vec(RWKV_DIM), vec(RWKV_DIM),
            vec(RWKV_DIM), mat(seg)],
        inputs=[zr] * n_zr + [zp0, s0, p["mu"], p["w0"], p["w_up"], p["a0"], p["a_up"], p["g_up"],
                              p["k_k"], p["k_a"], p["r_k"], p["gn_g"], p["gn_b"], seg],
        out_specs=[o_spec, pl.BlockSpec((None,) + st, lambda bi, j: (layer, bi, 0, 0, 0))],
        out_shape=[jax.ShapeDtypeStruct(mixed.shape, BF16),
                   jax.ShapeDtypeStruct((depth, batch) + st[1:], F32)],
        carried={0: mixed, 1: state},
        scratch_shapes=[pltpu.VMEM((nb, RWKV_HEADS // PACK_HEADS, PACK_HEADS * RWKV_HEAD,
                                    PACK_HEADS * RWKV_HEAD), F32),
                        pltpu.VMEM((nb, 1, RWKV_COLS), F32)],
        semantics=("parallel", "arbitrary"), name=name)


def _attn_body(q_ref, k_ref, v_ref, o_ref, *dense_refs, bb, tq):
    scale = MEM_HEAD_DIM ** -0.5
    pairs = [(i, h) for i in range(bb) for h in range(MEM_HEADS)]
    rows = lambda i: slice(i * tq, (i + 1) * tq)
    cols = lambda h: slice(h * MEM_HEAD_DIM, (h + 1) * MEM_HEAD_DIM)

    if dense_refs:
        for src, dst in zip((k_ref, v_ref), dense_refs):
            for i, h in pairs:
                dst[i, :, cols(h)] = src[i, :, h, :]
        k_ref, v_ref = dense_refs

    def head(ref, i, h):
        return ref[i, :, cols(h)].astype(BF16)

    s = [lax.dot_general(q_ref[rows(i), cols(h)], head(k_ref, i, h), _NT,
                         preferred_element_type=F32) * scale for i, h in pairs]
    e = [jnp.exp(x - jnp.max(x, axis=-1, keepdims=True)) for x in s]
    p = [x / jnp.sum(x, axis=-1, keepdims=True) for x in e]
    o = [_dot(p[n].astype(BF16), head(v_ref, i, h)) for n, (i, h) in enumerate(pairs)]
    for n, (i, h) in enumerate(pairs):
        o_ref[rows(i), cols(h)] = o[n].astype(o_ref.dtype)


def _attn_group(q, row0, batch, seq, tq, bb, mk, mv, *, layer, name):
    nt = seq // tq
    rows = bb * tq
    base = row0 // rows
    assert seq % tq == 0 and batch % bb == 0 and row0 % rows == 0 and (bb == 1 or nt == 1)
    if mk.ndim == 5:
        kv_spec = pl.BlockSpec((None, bb, N_MEM, MEM_HEADS, MEM_HEAD_DIM),
                               lambda bi, j: (layer, bi, 0, 0, 0))
        scratch = [pltpu.VMEM((bb, N_MEM, D_MODEL), F32)] * 2
    else:
        kv_spec = pl.BlockSpec((bb, N_MEM, D_MODEL), lambda bi, j: (bi, 0, 0))
        scratch = []
    return pl.pallas_call(
        functools.partial(_attn_body, bb=bb, tq=tq),
        grid=(batch // bb, nt),
        in_specs=[pl.BlockSpec((rows, D_MODEL), lambda bi, j: (base + bi * nt + j, 0)),
                  kv_spec, kv_spec],
        out_specs=pl.BlockSpec((rows, D_MODEL), lambda bi, j: (bi * nt + j, 0)),
        out_shape=jax.ShapeDtypeStruct((batch * seq, D_MODEL), BF16),
        scratch_shapes=scratch,
        compiler_params=_params(("parallel", "arbitrary")), name=name,
    )(q, mk, mv)


def kernel(x_prompt, x_sample, cache_mem_k, cache_mem_v, state_wkv, state_conv, state_shift,
           mem_prompt, w_in, mu_shift, conv_w, conv_b, conv_ln_g, conv_ln_b, rwkv_w0, rwkv_w_up,
           rwkv_a0, rwkv_a_up, rwkv_g_up, rwkv_k_k, rwkv_k_a, rwkv_r_k, rwkv_gn_g, rwkv_gn_b,
           w_out, mem_norm_g, w_q, w_k, w_v, w_o, w_ffn1, w_ffn2, norm_g):
    bp, tp, _ = x_prompt.shape
    bs, ts, _ = x_sample.shape
    depth = w_in.shape[0]
    n_p, n_s = bp * tp, bs * ts
    row = lambda a: a.reshape(1, -1)
    sample_nb = SAMPLE_STEP_ROWS // ts
    prompt_nb = min(RWKV_PROMPT_SEQS, bp)

    x = jnp.concatenate([x_prompt.reshape(n_p, D_MODEL), x_sample.reshape(n_s, D_MODEL)], axis=0)
    mem2d = mem_prompt.reshape(bp * N_MEM, D_MODEL)
    lane_head = jnp.arange(V7X_LANES) // RWKV_HEAD
    seg = (lane_head[:, None] == lane_head[None, :]).astype(BF16)
    zero_conv = jnp.zeros((bp, CONV_BUF, CONV_CH), F32)
    zero_prev = jnp.zeros((bp, 1, RWKV_COLS), F32)
    zero_wkv = jnp.zeros((bp, RWKV_HEADS, RWKV_HEAD, RWKV_HEAD), F32)

    in_splits = [(0, 2 * CONV_CH), (2 * CONV_CH, IN_COLS)]

    shift_p, shift_s = [], []
    mk5 = mv5 = wkv_p = conv_p = wkv_s = conv_s = None
    for l in range(depth):
        g = norm_g[l]
        w_in_b = w_in[l].astype(BF16)
        rp = dict(mu=row(mu_shift[l]), w0=row(rwkv_w0[l]), w_up=rwkv_w_up[l].astype(BF16),
                  a0=row(rwkv_a0[l]), a_up=rwkv_a_up[l].astype(BF16),
                  g_up=rwkv_g_up[l].astype(BF16), k_k=row(rwkv_k_k[l]), k_a=row(rwkv_k_a[l]),
                  r_k=row(rwkv_r_k[l]), gn_g=row(rwkv_gn_g[l]), gn_b=row(rwkv_gn_b[l]))
        cw, cb = conv_w[l], row(conv_b[l])
        clg, clb = row(conv_ln_g[l]), row(conv_ln_b[l])

        sub = V7X_SUBLANES
        tail_p = _rms_rows(x, g[0:1], tp - sub, tp, bp, sub, name="shift_prompt")
        shift_p.append(tail_p.reshape(bp, sub, D_MODEL)[:, sub - 1])
        tail_s = _rms_rows(x, g[0:1], n_p, n_s, 1, n_s, name="shift_sample")
        shift_s.append(tail_s.reshape(bs, ts, D_MODEL)[:, ts - 1])

        (zp_s,) = _norm_matmul(state_shift[l], g[0:1], w_in_b, in_splits[1:], out_dtype=F32,
                               normalize=False, name="prev_proj")

        zr_p, mixed_p, conv_p = _in_proj_conv(x, g[0:1], w_in_b, bp, tp, zero_conv, cw, cb, clg, clb,
                                              layer=l, depth=depth, state=conv_p)
        zc_s, zr_s = _norm_matmul(x, g[0:1], w_in_b, in_splits, out_dtype=F32, row0=n_p, n_rows=n_s,
                                  name="in_proj_sample")
        mixed_s, conv_s = _conv_group(zc_s, bs, ts, sample_nb, state_conv[l], cw, cb, clg, clb,
                                      layer=l, depth=depth, state=conv_s, name="conv_sample")
        mixed_p, wkv_p = _rwkv_group(zr_p, 0, bp, tp, prompt_nb, 1, RWKV_GROUP_ROWS, zero_prev,
                                     zero_wkv, rp, seg, layer=l, depth=depth, mixed=mixed_p,
                                     state=wkv_p, name="rwkv_prompt")
        mixed_s, wkv_s = _rwkv_group(zr_s, 0, bs, ts, 1, RWKV_GROUP_ROWS // ts, ts, zp_s[:, None, :],
                                     state_wkv[l], rp, seg, layer=l, depth=depth, mixed=mixed_s,
                                     state=wkv_s, name="rwkv_sample")

        x, q = _res_proj([mixed_p.reshape(n_p, D_MODEL), mixed_s], w_out[l].astype(BF16), x, g[1:2],
                         g_b=g[2:3], w_b=w_q[l].astype(BF16), splits=[(0, D_MODEL)], out_dtype=BF16,
                         name="out_q_proj")
        mk2, mv2, mk5, mv5 = _mem_kv_proj(mem2d, row(mem_norm_g), w_k[l].astype(BF16),
                                          w_v[l].astype(BF16), layer=l, depth=depth, k5=mk5, v5=mv5)
        att_p = _attn_group(q, 0, bp, tp, min(ATTN_TILE, tp), 1, mk2.reshape(bp, N_MEM, D_MODEL),
                            mv2.reshape(bp, N_MEM, D_MODEL), layer=l, name="attn_prompt")
        att_s = _attn_group(q, n_p, bs, ts, ts, ATTN_SAMPLE_BATCH, cache_mem_k, cache_mem_v,
                            layer=l, name="attn_sample")
        x, f = _res_proj([att_p, att_s], w_o[l].astype(BF16), x, g[3:4], g_b=g[4:5],
                         w_b=w_ffn1[l].astype(BF16), splits=[(0, D_FF)], relu2=True, out_dtype=BF16,
                         name="attn_out_ffn_up")
        w_down = w_ffn2[l].astype(BF16)
        if l + 1 < depth:
            (x,) = _res_proj([f], w_down, x, g[5:6], name="ffn_down")
        else:
            (y_p,) = _res_proj([f], w_down, x, g[5:6], row0=0, n_rows=n_p, name="ffn_down_prompt")
            (y_s,) = _res_proj([f], w_down, x, g[5:6], row0=n_p, n_rows=n_s, name="ffn_down_sample")

    return (y_p.reshape(bp, tp, D_MODEL), y_s.reshape(bs, ts, D_MODEL), mk5, mv5, wkv_p, conv_p,
            jnp.stack(shift_p), wkv_s, conv_s, jnp.stack(shift_s))
```

```python
import functools
import math

import jax
import jax.numpy as jnp
from jax import lax
from jax.experimental import pallas as pl
from jax.experimental.pallas import tpu as pltpu

F32 = jnp.float32
BF16 = jnp.bfloat16

D_MODEL = 1024
CONV_CH = 512
RWKV_DIM = 512
RWKV_HEAD = 64
RWKV_HEADS = 8
CONV_WIDTH = 31
CONV_BUF = CONV_WIDTH - 1
DECAY_RANK = 64
ICLR_RANK = 64
GATE_RANK = 128
RWKV_COLS = 3 * RWKV_DIM + DECAY_RANK + ICLR_RANK + GATE_RANK
IN_COLS = 2 * CONV_CH + RWKV_COLS
N_MEM = 256
MEM_HEADS = 4
MEM_HEAD_DIM = D_MODEL // MEM_HEADS
D_FF = 4 * D_MODEL
RMS_EPS = 1e-6
LN_EPS = 1e-5
GN_EPS = 64e-5

V7X_LANES = 128
V7X_SUBLANES = 8
V7X_VMEM_LIMIT_BYTES = 56 * 1024 * 1024

ROW_BLOCK = 512
COL_CHUNK = 1024
RWKV_GROUP_ROWS = 64
RWKV_PROMPT_SEQS = 4
PACK_HEADS = 4
SAMPLE_STEP_ROWS = 64
INV_BASE = 8
ATTN_TILE = 512
ATTN_SAMPLE_BATCH = 4

_NT = (((1,), (1,)), ((), ()))
_TN = (((0,), (0,)), ((), ()))


def _params(semantics):
    return pltpu.CompilerParams(dimension_semantics=semantics,
                                vmem_limit_bytes=V7X_VMEM_LIMIT_BYTES)


def _dot(a, b):
    return jnp.dot(a, b, preferred_element_type=F32)


def _mm(a, b, dims=None):
    if dims is None:
        dims = (((a.ndim - 1,), (0,)), ((), ()))
    return lax.dot_general(a.astype(BF16), b.astype(BF16), dims, preferred_element_type=F32)


def _split2(x):
    hi = x.astype(BF16)
    lo = (x - hi.astype(F32)).astype(BF16)
    return hi, lo


def _split3(x):
    hi = x.astype(BF16)
    r1 = x - hi.astype(F32)
    mid = r1.astype(BF16)
    lo = (r1 - mid.astype(F32)).astype(BF16)
    return hi, mid, lo


def _rms(x, g):
    ms = jnp.mean(x * x, axis=-1, keepdims=True)
    return x * lax.rsqrt(ms + RMS_EPS) * g


def _drop_carried(body, n_in, n_carried):
    def wrapped(*refs):
        return body(*refs[:n_in], *refs[n_in + n_carried:])
    return wrapped


def _call_with_carried(body, *, grid, in_specs, inputs, out_specs, out_shape, carried,
                       scratch_shapes=(), semantics, name):
    n_in = len(inputs)
    extra, aliases = [], {}
    for out_idx, buf in sorted(carried.items()):
        if buf is None:
            continue
        aliases[n_in + len(extra)] = out_idx
        extra.append(buf)
    return pl.pallas_call(
        _drop_carried(body, n_in, len(extra)),
        grid=grid,
        in_specs=list(in_specs) + [pl.BlockSpec(memory_space=pl.ANY)] * len(extra),
        out_specs=out_specs, out_shape=out_shape, scratch_shapes=list(scratch_shapes),
        input_output_aliases=aliases, compiler_params=_params(semantics), name=name,
    )(*inputs, *extra)


def _resident(shape):
    return pl.BlockSpec(shape, lambda i: (0,) * len(shape), pipeline_mode=pl.Buffered(1))


def _weight_spec(w, layer):
    if w.ndim == 2:
        return _resident(w.shape)
    return pl.BlockSpec((None,) + w.shape[1:], lambda i: (layer, 0, 0), pipeline_mode=pl.Buffered(1))


def _project(hb, w_ref, o_refs, splits, relu2):
    for (lo, hi), o_ref in zip(splits, o_refs):
        for c0 in range(lo, hi, COL_CHUNK):
            c1 = min(hi, c0 + COL_CHUNK)
            y = _dot(hb, w_ref[:, c0:c1])
            if relu2:
                y = jnp.square(jnp.maximum(y, 0.0))
            o_ref[:, c0 - lo:c1 - lo] = y.astype(o_ref.dtype)


def _norm_matmul_body(x_ref, g_ref, w_ref, *o_refs, splits, normalize, relu2):
    x = x_ref[...]
    h = _rms(x, g_ref[...]) if normalize else x
    _project(h.astype(BF16), w_ref, o_refs, splits, relu2)


def _norm_matmul(x, g, w, splits, *, out_dtype, normalize=True, relu2=False, row0=0, n_rows=None,
                 layer=None, name):
    k = x.shape[1]
    m = x.shape[0] if n_rows is None else n_rows
    bm = math.gcd(ROW_BLOCK, m, row0)
    return pl.pallas_call(
        functools.partial(_norm_matmul_body, splits=tuple(splits), normalize=normalize,
                          relu2=relu2),
        grid=(m // bm,),
        in_specs=[pl.BlockSpec((bm, k), lambda i: (row0 // bm + i, 0)), _resident((1, k)),
                  _weight_spec(w, layer)],
        out_specs=[pl.BlockSpec((bm, hi - lo), lambda i: (i, 0)) for lo, hi in splits],
        out_shape=[jax.ShapeDtypeStruct((m, hi - lo), out_dtype) for lo, hi in splits],
        compiler_params=_params(("parallel",)), name=name,
    )(x, g, w)


def _res_proj_body(*refs, n_lhs, first_blocks, splits, relu2):
    y_refs = refs[:n_lhs]
    w_a_ref, x_ref, g_a_ref = refs[n_lhs:n_lhs + 3]
    rest = refs[n_lhs + 3:]
    if splits:
        g_b_ref, w_b_ref = rest[:2]
        rest = rest[2:]
    o_ref, p_refs = rest[0], rest[1:]

    def emit(y_ref):
        x_new = x_ref[...] + _rms(_dot(y_ref[...], w_a_ref[...]), g_a_ref[...])
        o_ref[...] = x_new
        if splits:
            _project(_rms(x_new, g_b_ref[...]).astype(BF16), w_b_ref, p_refs, splits, relu2)

    if n_lhs == 1:
        emit(y_refs[0])
    else:
        i = pl.program_id(0)
        pl.when(i < first_blocks)(lambda: emit(y_refs[0]))
        pl.when(i >= first_blocks)(lambda: emit(y_refs[1]))


def _res_proj(lhs, w_a, x, g_a, *, g_b=None, w_b=None, splits=(), relu2=False, out_dtype=None,
              row0=0, n_rows=None, layer=None, name):
    m, n = x.shape
    k = w_a.shape[-2]
    n_rows = m if n_rows is None else n_rows
    bm = math.gcd(ROW_BLOCK, n_rows, row0, *[a.shape[0] for a in lhs])
    if len(lhs) == 1:
        lhs_specs = [pl.BlockSpec((bm, k), lambda i: (row0 // bm + i, 0))]
        first_blocks = None
    else:
        assert row0 == 0 and n_rows == m and sum(a.shape[0] for a in lhs) == m
        first_blocks = lhs[0].shape[0] // bm
        lhs_specs = [pl.BlockSpec((bm, k), lambda i: (jnp.minimum(i, first_blocks - 1), 0)),
                     pl.BlockSpec((bm, k), lambda i: (jnp.maximum(i - first_blocks, 0), 0))]
    splits = tuple(splits)
    row_out = lambda width: pl.BlockSpec((bm, width), lambda i: (i, 0))
    proj_in = [_resident((1, n)), _weight_spec(w_b, layer)] if splits else []
    proj_args = [g_b, w_b] if splits else []
    return pl.pallas_call(
        functools.partial(_res_proj_body, n_lhs=len(lhs), first_blocks=first_blocks,
                          splits=splits, relu2=relu2),
        grid=(n_rows // bm,),
        in_specs=lhs_specs + [_weight_spec(w_a, layer),
                              pl.BlockSpec((bm, n), lambda i: (row0 // bm + i, 0)),
                              _resident((1, n))] + proj_in,
        out_specs=[row_out(n)] + [row_out(hi - lo) for lo, hi in splits],
        out_shape=[jax.ShapeDtypeStruct((n_rows, n), F32)]
        + [jax.ShapeDtypeStruct((n_rows, hi - lo), out_dtype) for lo, hi in splits],
        compiler_params=_params(("parallel",)), name=name,
    )(*lhs, w_a, x, g_a, *proj_args)


def _rms_rows_body(x_ref, g_ref, o_ref):
    o_ref[...] = _rms(x_ref[...], g_ref[...])


def _rms_rows(x, g, row0, stride, n_blocks, bm, *, name):
    k = x.shape[1]
    assert row0 % bm == 0 and stride % bm == 0
    return pl.pallas_call(
        _rms_rows_body, grid=(n_blocks,),
        in_specs=[pl.BlockSpec((bm, k), lambda i: (row0 // bm + i * (stride // bm), 0)),
                  _resident((1, k))],
        out_specs=pl.BlockSpec((bm, k), lambda i: (i, 0)),
        out_shape=jax.ShapeDtypeStruct((n_blocks * bm, k), F32),
        compiler_params=_params(("parallel",)), name=name,
    )(x, g)


def _mem_kv_body(x_ref, g_ref, wk_ref, wv_ref, k2_ref, v2_ref, k5_ref, v5_ref, *, nb):
    hb = _rms(x_ref[...], g_ref[...]).astype(BF16)
    for w_ref, o2_ref, o5_ref in ((wk_ref, k2_ref, k5_ref), (wv_ref, v2_ref, v5_ref)):
        y = _dot(hb, w_ref[...])
        o2_ref[...] = y.astype(o2_ref.dtype)
        for i in range(nb):
            for h in range(MEM_HEADS):
                o5_ref[i, :, h, :] = y[i * N_MEM:(i + 1) * N_MEM,
                                       h * MEM_HEAD_DIM:(h + 1) * MEM_HEAD_DIM]


def _mem_kv_proj(mem2d, g, wk, wv, *, layer, depth, k5, v5):
    m, k = mem2d.shape
    nb = ROW_BLOCK // N_MEM
    batch = m // N_MEM
    assert batch % nb == 0
    row_spec = pl.BlockSpec((nb * N_MEM, D_MODEL), lambda i: (i, 0))
    w_spec = _weight_spec(wk, layer)
    out5 = pl.BlockSpec((None, nb, N_MEM, MEM_HEADS, MEM_HEAD_DIM), lambda i: (layer, i, 0, 0, 0))
    shape5 = jax.ShapeDtypeStruct((depth, batch, N_MEM, MEM_HEADS, MEM_HEAD_DIM), F32)
    return _call_with_carried(
        functools.partial(_mem_kv_body, nb=nb),
        grid=(batch // nb,),
        in_specs=[row_spec, pl.BlockSpec((1, k), lambda i: (0, 0)), w_spec, w_spec],
        inputs=[mem2d, g, wk, wv],
        out_specs=[row_spec, row_spec, out5, out5],
        out_shape=[jax.ShapeDtypeStruct((m, D_MODEL), BF16), jax.ShapeDtypeStruct((m, D_MODEL), BF16),
                   shape5, shape5],
        carried={2: k5, 3: v5}, semantics=("parallel",), name="mem_kv_proj")


CONV_CUR_ROW = CONV_BUF + (-CONV_BUF) % V7X_SUBLANES
CONV_HALO_ROW = CONV_CUR_ROW - CONV_BUF


def _conv_tile(z, ext_ref, sh_ref, w_ref, b_ref, lg_ref, lb_ref):
    tt = z.shape[0]
    h0, cur = CONV_HALO_ROW, CONV_CUR_ROW
    ext_ref[cur:cur + tt, :] = z[:, :CONV_CH] * jax.nn.sigmoid(z[:, CONV_CH:])
    acc = None
    for r in range(V7X_SUBLANES):
        taps = range(r, CONV_WIDTH, V7X_SUBLANES)
        span = tt + taps[-1] - r
        sh_ref[r, 0:span, :] = ext_ref[h0 + r:h0 + r + span, :]
        for k in taps:
            term = sh_ref[r, k - r:k - r + tt, :] * w_ref[k:k + 1, :]
            acc = term if acc is None else acc + term
    y = acc + b_ref[...]
    mu = jnp.mean(y, axis=-1, keepdims=True)
    var = jnp.mean(jnp.square(y - mu), axis=-1, keepdims=True)
    yn = (y - mu) * lax.rsqrt(var + LN_EPS) * lg_ref[...] + lb_ref[...]
    return yn * jax.nn.sigmoid(yn), ext_ref[tt + h0:tt + cur, :]


def _conv_scratch(tt):
    return [pltpu.VMEM((CONV_CUR_ROW + tt, CONV_CH), F32),
            pltpu.VMEM((V7X_SUBLANES, tt + CONV_BUF, CONV_CH), F32)]


def _conv_body(z_ref, buf_ref, w_ref, b_ref, lg_ref, lb_ref, y_ref, nc_ref, ext_ref, sh_ref,
               *, nb, tt):
    for i in range(nb):
        rows = slice(i * tt, (i + 1) * tt)
        ext_ref[CONV_HALO_ROW:CONV_CUR_ROW, :] = buf_ref[i]
        out, halo = _conv_tile(z_ref[rows, :], ext_ref, sh_ref, w_ref, b_ref, lg_ref, lb_ref)
        y_ref[rows, :] = out.astype(y_ref.dtype)
        nc_ref[i] = halo


def _conv_group(zc, batch, seq, nb, conv_buf, w, b, lg, lb, *, layer, depth, state, name):
    rows = nb * seq
    assert batch % nb == 0 and zc.shape[0] == batch * seq
    const = lambda bi: (0, 0)
    return _call_with_carried(
        functools.partial(_conv_body, nb=nb, tt=seq),
        grid=(batch // nb,),
        in_specs=[pl.BlockSpec((rows, 2 * CONV_CH), lambda bi: (bi, 0)),
                  pl.BlockSpec((nb, CONV_BUF, CONV_CH), lambda bi: (bi, 0, 0)),
                  pl.BlockSpec((CONV_WIDTH, CONV_CH), const),
                  pl.BlockSpec((1, CONV_CH), const), pl.BlockSpec((1, CONV_CH), const),
                  pl.BlockSpec((1, CONV_CH), const)],
        inputs=[zc, conv_buf, w, b, lg, lb],
        out_specs=[pl.BlockSpec((rows, CONV_CH), lambda bi: (bi, 0)),
                   pl.BlockSpec((None, nb, CONV_BUF, CONV_CH), lambda bi: (layer, bi, 0, 0))],
        out_shape=[jax.ShapeDtypeStruct((batch * seq, CONV_CH + RWKV_DIM), BF16),
                   jax.ShapeDtypeStruct((depth, batch, CONV_BUF, CONV_CH), F32)],
        carried={1: state}, scratch_shapes=_conv_scratch(seq),
        semantics=("parallel",), name=name)


def _in_proj_conv_body(x_ref, g_ref, w_ref, buf_ref, cw_ref, cb_ref, lg_ref, lb_ref,
                       zr_ref, y_ref, nc_ref, ext_ref, sh_ref, *, tiles_per_seq):
    @pl.when(pl.program_id(0) % tiles_per_seq == 0)
    def _():
        ext_ref[CONV_HALO_ROW:CONV_CUR_ROW, :] = buf_ref[0]

    hb = _rms(x_ref[...], g_ref[...]).astype(BF16)
    z = _dot(hb, w_ref[:, 0:2 * CONV_CH])
    out, halo = _conv_tile(z, ext_ref, sh_ref, cw_ref, cb_ref, lg_ref, lb_ref)
    y_ref[...] = out.astype(y_ref.dtype)
    ext_ref[CONV_HALO_ROW:CONV_CUR_ROW, :] = halo
    nc_ref[0] = halo
    _project(hb, w_ref, [zr_ref], [(2 * CONV_CH, IN_COLS)], False)


def _in_proj_conv(x, g, w, batch, seq, conv_buf, cw, cb, lg, lb, *, layer, depth, state):
    k = x.shape[1]
    tt = math.gcd(ROW_BLOCK, seq)
    tiles = seq // tt
    const = lambda i: (0, 0)
    return _call_with_carried(
        functools.partial(_in_proj_conv_body, tiles_per_seq=tiles),
        grid=(batch * tiles,),
        in_specs=[pl.BlockSpec((tt, k), lambda i: (i, 0)), _resident((1, k)), _weight_spec(w, layer),
                  pl.BlockSpec((1, CONV_BUF, CONV_CH), lambda i: (i // tiles, 0, 0)),
                  pl.BlockSpec((CONV_WIDTH, CONV_CH), const),
                  pl.BlockSpec((1, CONV_CH), const), pl.BlockSpec((1, CONV_CH), const),
                  pl.BlockSpec((1, CONV_CH), const)],
        inputs=[x, g, w, conv_buf, cw, cb, lg, lb],
        out_specs=[pl.BlockSpec((tt, RWKV_COLS), lambda i: (i, 0)),
                   pl.BlockSpec((None, tt, CONV_CH), lambda i: (i // tiles, i % tiles, 0)),
                   pl.BlockSpec((None, 1, CONV_BUF, CONV_CH), lambda i: (layer, i // tiles, 0, 0))],
        out_shape=[jax.ShapeDtypeStruct((batch * seq, RWKV_COLS), F32),
                   jax.ShapeDtypeStruct((batch, seq, CONV_CH + RWKV_DIM), BF16),
                   jax.ShapeDtypeStruct((depth, batch, CONV_BUF, CONV_CH), F32)],
        carried={2: state}, scratch_shapes=_conv_scratch(tt),
        semantics=("arbitrary",), name="in_proj_conv")


def _head_sums(xs, seg):
    rows = xs[0].shape[0]
    groups = RWKV_DIM // V7X_LANES
    stacked = jnp.concatenate(
        [x[:, g * V7X_LANES:(g + 1) * V7X_LANES] for x in xs for g in range(groups)], axis=0)
    hi, lo = _split2(stacked)
    y = _dot(hi, seg) + _dot(lo, seg)
    outs = []
    for i in range(len(xs)):
        parts = [y[(i * groups + g) * rows:(i * groups + g + 1) * rows] for g in range(groups)]
        outs.append(jnp.concatenate(parts, axis=1))
    return outs


def _block_diag(blocks):
    n = len(blocks)
    w = blocks[0].shape[1]
    rows = []
    for i, blk in enumerate(blocks):
        pieces = ([jnp.zeros((blk.shape[0], i * w), blk.dtype)] if i else []) + [blk]
        if i < n - 1:
            pieces.append(jnp.zeros((blk.shape[0], (n - 1 - i) * w), blk.dtype))
        rows.append(jnp.concatenate(pieces, axis=1))
    return jnp.concatenate(rows, axis=0)


def _rwkv_body(*refs, n_grp, spg, tlen, n_zr):
    zr_refs = refs[:n_zr]
    (zp0_ref, s0_ref, mu_ref, w0_ref, wup_ref, a0_ref, aup_ref, gup_ref, kk_ref, ka_ref, rk_ref,
     gng_ref, gnb_ref, seg_ref, o_ref, s_out_ref, s_scr, carry_scr) = refs[n_zr:]
    nb = n_grp * spg
    rows = nb * tlen
    gr = spg * tlen
    shift = int(math.log2(tlen))
    n_pack = RWKV_HEADS // PACK_HEADS
    pw = PACK_HEADS * RWKV_HEAD
    assert 1 << shift == tlen and gr == RWKV_HEAD == RWKV_GROUP_ROWS

    @pl.when(pl.program_id(1) == 0)
    def _():
        carry_scr[...] = zp0_ref[...]
        for q in range(nb):
            for p in range(n_pack):
                s_scr[q, p] = _block_diag([s0_ref[q, p * PACK_HEADS + h] for h in range(PACK_HEADS)])

    if n_zr == 1:
        zr = zr_refs[0][...]
    else:
        zr = jnp.concatenate([r[...] for r in zr_refs], axis=0)
    seqs = [slice(i * tlen, (i + 1) * tlen) for i in range(nb)]
    t_idx = lax.broadcasted_iota(jnp.int32, (rows, 1), 0) & (tlen - 1)
    if nb == 1:
        carry_rows = carry_scr[0]
    else:
        carry_rows = jnp.concatenate(
            [jnp.broadcast_to(carry_scr[i], (tlen, RWKV_COLS)) for i in range(nb)], axis=0)
    prev = jnp.where(t_idx == 0, carry_rows, pltpu.roll(zr, 1, axis=0))
    for i, rs in enumerate(seqs):
        carry_scr[i] = zr[rs.stop - 1:rs.stop, :]
    zs = zr + (prev - zr) * mu_ref[...]

    d = RWKV_DIM
    r = zs[:, 0:d]
    k = zs[:, d:2 * d]
    v = zs[:, 2 * d:3 * d]
    dw = zs[:, 3 * d:3 * d + DECAY_RANK]
    da = zs[:, 3 * d + DECAY_RANK:3 * d + DECAY_RANK + ICLR_RANK]
    dg = zs[:, 3 * d + DECAY_RANK + ICLR_RANK:]

    lw = w0_ref[...] + _mm(jnp.tanh(dw), wup_ref[...])
    logw = -math.exp(-0.5) * jax.nn.sigmoid(lw)
    a = jax.nn.sigmoid(a0_ref[...] + _mm(da, aup_ref[...]))
    gate = _mm(jax.nn.sigmoid(dg), gup_ref[...])

    seg = seg_ref[...]
    kk = k * kk_ref[...]
    kmod = k * (1.0 + (a - 1.0) * ka_ref[...])
    kk_sq, bonus_dot = _head_sums([kk * kk, r * kmod * rk_ref[...]], seg)
    kk = kk / jnp.maximum(jnp.sqrt(kk_sq), 1e-12)
    bonus = bonus_dot * v

    ri = lax.broadcasted_iota(jnp.int32, (rows, rows), 0)
    ci = lax.broadcasted_iota(jnp.int32, (rows, rows), 1)
    tri = (((ri >> shift) == (ci >> shift)) & (ri >= ci)).astype(BF16)
    w_hi, w_mid, w_lo = _split3(logw)
    cl = _dot(tri, w_hi) + _dot(tri, w_mid) + _dot(tri, w_lo)
    g_incl = jnp.exp(cl)
    g_inv = jnp.exp(-cl)
    a_t = -kk * jnp.exp(cl - logw)
    b_t = kk * a * g_inv
    k_t = kmod * g_inv
    r_t = r * g_incl

    pr = lax.broadcasted_iota(jnp.int32, (gr, pw), 0)
    pc = lax.broadcasted_iota(jnp.int32, (gr, pw), 1) & (RWKV_HEAD - 1)
    p_same = (pr >> shift) == (pc >> shift)
    p_incl = p_same & (pr >= pc)
    p_strict = p_same & (pr > pc)
    p_eye = (pr == pc).astype(F32)
    base_shift = min(shift, int(math.log2(INV_BASE)))
    base_blk = (pr >> base_shift) == (pc >> base_shift)
    level_blks = [((pr >> (lv + 1)) == (pc >> (lv + 1))) & ((pr >> lv) != (pc >> lv))
                  for lv in range(base_shift, shift)]
    bd_mask = ((lax.broadcasted_iota(jnp.int32, (pw, pw), 0) >> 6)
               == (lax.broadcasted_iota(jnp.int32, (pw, pw), 1) >> 6))
    bd_zero = jnp.zeros((pw, pw), BF16)

    def bd(x):
        return jnp.where(bd_mask, jnp.concatenate([x.astype(BF16)] * PACK_HEADS, axis=0), bd_zero)

    def pk(x, y_bd, dims=None):
        return _mm(x, y_bd, dims)

    chains = [(g, p) for g in range(n_grp) for p in range(n_pack)]
    nch = range(len(chains))
    cut = lambda m, g, p: m[g * gr:(g + 1) * gr, p * pw:(p + 1) * pw]
    a_p = [cut(a_t, g, p) for g, p in chains]
    b_p = [cut(b_t, g, p) for g, p in chains]
    k_p = [cut(k_t, g, p) for g, p in chains]
    r_p = [cut(r_t, g, p) for g, p in chains]
    v_p = [cut(v, g, p) for g, p in chains]
    sub = [slice(j * tlen, (j + 1) * tlen) for j in range(spg)]
    ar = [jnp.concatenate([a_p[c], r_p[c]], axis=0) for c in nch]
    b_bd = [bd(b_p[c]) for c in nch]
    k_bd = [bd(k_p[c]) for c in nch]
    v_bd = [bd(v_p[c]) for c in nch]
    gb = [pk(ar[c], b_bd[c], _NT) for c in nch]
    gk = [pk(ar[c], k_bd[c], _NT) for c in nch]
    a_ab = [jnp.where(p_strict, gb[c][:gr], 0.0) for c in nch]
    a_rb = [jnp.where(p_incl, gb[c][gr:], 0.0) for c in nch]
    a_ak = [jnp.where(p_strict, gk[c][:gr], 0.0) for c in nch]
    a_rk = [jnp.where(p_incl, gk[c][gr:], 0.0) for c in nch]
    s_old = [[s_scr[g * spg + j, p] for j in range(spg)] for g, p in chains]
    s_terms = [[pk(jnp.concatenate([a_p[c][sr], r_p[c][sr]], axis=0), s_old[c][j], _NT)
                for j, sr in enumerate(sub)] for c in nch]
    a_s0 = [jnp.concatenate([x[:tlen] for x in s_terms[c]], axis=0) for c in nch]
    r_s0 = [jnp.concatenate([x[tlen:] for x in s_terms[c]], axis=0) for c in nch]
    av = [pk(jnp.concatenate([a_ak[c], a_rk[c]], axis=0), v_bd[c]) for c in nch]
    rhs = [a_s0[c] + av[c][:gr] for c in nch]
    npow = [jnp.where(base_blk, a_ab[c], 0.0) for c in nch]
    inv = [p_eye + npow[c] for c in nch]
    for _ in range(max(0, base_shift - 1)):
        npow = [pk(npow[c], bd(npow[c])) for c in nch]
        inv = [inv[c] + pk(inv[c], bd(npow[c])) for c in nch]
    for off_blk in level_blks:
        cross = [pk(jnp.where(off_blk, a_ab[c], 0.0), bd(inv[c])) for c in nch]
        inv = [inv[c] + pk(inv[c], bd(cross[c])) for c in nch]
    u = [pk(inv[c], bd(rhs[c])) for c in nch]
    outs = [r_s0[c] + pk(a_rb[c], bd(u[c])) + av[c][gr:] for c in nch]
    for c, (g, p) in enumerate(chains):
        for j, sr in enumerate(sub):
            uv = jnp.concatenate([u[c][sr], v_p[c][sr]], axis=0)
            bk = jnp.concatenate([b_p[c][sr], k_p[c][sr]], axis=0)
            last_row = g * gr + sr.stop - 1
            g_last = g_incl[last_row:last_row + 1, p * pw:(p + 1) * pw]
            upd = jnp.where(bd_mask, pk(uv, bk, _TN), 0.0)
            s_scr[g * spg + j, p] = (s_old[c][j] + upd) * g_last
    o = jnp.concatenate(
        [jnp.concatenate([outs[g * n_pack + p] for p in range(n_pack)], axis=1)
         for g in range(n_grp)], axis=0)

    inv_n = 1.0 / RWKV_HEAD
    (o_sum,) = _head_sums([o], seg)
    cen = o - o_sum * inv_n
    (sq_sum,) = _head_sums([cen * cen], seg)
    y = cen * lax.rsqrt(sq_sum * inv_n + GN_EPS) * gng_ref[...] + gnb_ref[...] + bonus
    y = (y * gate).astype(o_ref.dtype)
    if len(o_ref.shape) == 3:
        for i, rs in enumerate(seqs):
            o_ref[i] = y[rs]
    else:
        o_ref[...] = y

    @pl.when(pl.program_id(1) == pl.num_programs(1) - 1)
    def _():
        for q in range(nb):
            for h in range(RWKV_HEADS):
                blk = slice((h % PACK_HEADS) * RWKV_HEAD, (h % PACK_HEADS + 1) * RWKV_HEAD)
                s_out_ref[q, h] = s_scr[q, h // PACK_HEADS, blk, blk]


def _rwkv_group(zr, row0, batch, seq, n_grp, spg, tlen, zp0, s0, p, seg, *, layer, depth, mixed,
                state, name):
    nb = n_grp * spg
    nc = seq // tlen
    assert seq % tlen == 0 and batch % nb == 0 and row0 % tlen == 0 and (spg == 1 or nc == 1)
    const = lambda bi, j: (0, 0)
    vec = lambda n: pl.BlockSpec((1, n), const)
    mat = lambda a: pl.BlockSpec(a.shape, const)
    st = (nb, RWKV_HEADS, RWKV_HEAD, RWKV_HEAD)
    base = row0 // tlen
    if nc > 1:
        zr_specs = [pl.BlockSpec((tlen, RWKV_COLS),
                                 lambda bi, j, i=i: (base + (bi * nb + i) * nc + j, 0))
                    for i in range(nb)]
    else:
        assert row0 % (nb * tlen) == 0
        zr_specs = [pl.BlockSpec((nb * tlen, RWKV_COLS),
                                 lambda bi, j: (row0 // (nb * tlen) + bi, 0))]
    if mixed.ndim == 3:
        o_spec = pl.BlockSpec((nb, tlen, RWKV_DIM), lambda bi, j: (bi, j, 1))
    else:
        assert nc == 1
        o_spec = pl.BlockSpec((nb * tlen, RWKV_DIM), lambda bi, j: (bi, 1))
    n_zr = len(zr_specs)
    return _call_with_carried(
        functools.partial(_rwkv_body, n_grp=n_grp, spg=spg, tlen=tlen, n_zr=n_zr),
        grid=(batch // nb, nc),
        in_specs=zr_specs + [
            pl.BlockSpec((nb, 1, RWKV_COLS), lambda bi, j: (bi, 0, 0)),
            pl.BlockSpec(st, lambda bi, j: (bi, 0, 0, 0)),
            vec(RWKV_COLS), vec(RWKV_DIM), mat(p["w_up"]), vec(RWKV_DIM), mat(p["a_up"]),
            mat(p["g_up"]), vec(RWKV_DIM), vec(RWKV_DIM), vec(RWKV_DIM), vec(RWKV_DIM),
            vec(RWKV_DIM), mat(seg)],
        inputs=[zr] * n_zr + [zp0, s0, p["mu"], p["w0"], p["w_up"], p["a0"], p["a_up"], p["g_up"],
                              p["k_k"], p["k_a"], p["r_k"], p["gn_g"], p["gn_b"], seg],
        out_specs=[o_spec, pl.BlockSpec((None,) + st, lambda bi, j: (layer, bi, 0, 0, 0))],
        out_shape=[jax.ShapeDtypeStruct(mixed.shape, BF16),
                   jax.ShapeDtypeStruct((depth, batch) + st[1:], F32)],
        carried={0: mixed, 1: state},
        scratch_shapes=[pltpu.VMEM((nb, RWKV_HEADS // PACK_HEADS, PACK_HEADS * RWKV_HEAD,
                                    PACK_HEADS * RWKV_HEAD), F32),
                        pltpu.VMEM((nb, 1, RWKV_COLS), F32)],
        semantics=("parallel", "arbitrary"), name=name)


def _attn_body(q_ref, k_ref, v_ref, o_ref, *dense_refs, bb, tq):
    scale = MEM_HEAD_DIM ** -0.5
    pairs = [(i, h) for i in range(bb) for h in range(MEM_HEADS)]
    rows = lambda i: slice(i * tq, (i + 1) * tq)
    cols = lambda h: slice(h * MEM_HEAD_DIM, (h + 1) * MEM_HEAD_DIM)

    if dense_refs:
        for src, dst in zip((k_ref, v_ref), dense_refs):
            for i, h in pairs:
                dst[i, :, cols(h)] = src[i, :, h, :]
        k_ref, v_ref = dense_refs

    def head(ref, i, h):
        return ref[i, :, cols(h)].astype(BF16)

    s = [lax.dot_general(q_ref[rows(i), cols(h)], head(k_ref, i, h), _NT,
                         preferred_element_type=F32) * scale for i, h in pairs]
    e = [jnp.exp(x - jnp.max(x, axis=-1, keepdims=True)) for x in s]
    p = [x / jnp.sum(x, axis=-1, keepdims=True) for x in e]
    o = [_dot(p[n].astype(BF16), head(v_ref, i, h)) for n, (i, h) in enumerate(pairs)]
    for n, (i, h) in enumerate(pairs):
        o_ref[rows(i), cols(h)] = o[n].astype(o_ref.dtype)


def _attn_group(q, row0, batch, seq, tq, bb, mk, mv, *, layer, name):
    nt = seq // tq
    rows = bb * tq
    base = row0 // rows
    assert seq % tq == 0 and batch % bb == 0 and row0 % rows == 0 and (bb == 1 or nt == 1)
    if mk.ndim == 5:
        kv_spec = pl.BlockSpec((None, bb, N_MEM, MEM_HEADS, MEM_HEAD_DIM),
                               lambda bi, j: (layer, bi, 0, 0, 0))
        scratch = [pltpu.VMEM((bb, N_MEM, D_MODEL), F32)] * 2
    else:
        kv_spec = pl.BlockSpec((bb, N_MEM, D_MODEL), lambda bi, j: (bi, 0, 0))
        scratch = []
    return pl.pallas_call(
        functools.partial(_attn_body, bb=bb, tq=tq),
        grid=(batch // bb, nt),
        in_specs=[pl.BlockSpec((rows, D_MODEL), lambda bi, j: (base + bi * nt + j, 0)),
                  kv_spec, kv_spec],
        out_specs=pl.BlockSpec((rows, D_MODEL), lambda bi, j: (bi * nt + j, 0)),
        out_shape=jax.ShapeDtypeStruct((batch * seq, D_MODEL), BF16),
        scratch_shapes=scratch,
        compiler_params=_params(("parallel", "arbitrary")), name=name,
    )(q, mk, mv)


def kernel(x_prompt, x_sample, cache_mem_k, cache_mem_v, state_wkv, state_conv, state_shift,
           mem_prompt, w_in, mu_shift, conv_w, conv_b, conv_ln_g, conv_ln_b, rwkv_w0, rwkv_w_up,
           rwkv_a0, rwkv_a_up, rwkv_g_up, rwkv_k_k, rwkv_k_a, rwkv_r_k, rwkv_gn_g, rwkv_gn_b,
           w_out, mem_norm_g, w_q, w_k, w_v, w_o, w_ffn1, w_ffn2, norm_g):
    bp, tp, _ = x_prompt.shape
    bs, ts, _ = x_sample.shape
    depth = w_in.shape[0]
    n_p, n_s = bp * tp, bs * ts
    row = lambda a: a.reshape(1, -1)
    sample_nb = SAMPLE_STEP_ROWS // ts
    prompt_nb = min(RWKV_PROMPT_SEQS, bp)

    x = jnp.concatenate([x_prompt.reshape(n_p, D_MODEL), x_sample.reshape(n_s, D_MODEL)], axis=0)
    mem2d = mem_prompt.reshape(bp * N_MEM, D_MODEL)
    lane_head = jnp.arange(V7X_LANES) // RWKV_HEAD
    seg = (lane_head[:, None] == lane_head[None, :]).astype(BF16)
    zero_conv = jnp.zeros((bp, CONV_BUF, CONV_CH), F32)
    zero_prev = jnp.zeros((bp, 1, RWKV_COLS), F32)
    zero_wkv = jnp.zeros((bp, RWKV_HEADS, RWKV_HEAD, RWKV_HEAD), F32)

    in_splits = [(0, 2 * CONV_CH), (2 * CONV_CH, IN_COLS)]
    w_in_b, w_out_b, w_q_b, w_k_b, w_v_b, w_o_b, w_up_b, w_down_b = (
        w.astype(BF16) for w in (w_in, w_out, w_q, w_k, w_v, w_o, w_ffn1, w_ffn2))

    shift_p, shift_s = [], []
    mk5 = mv5 = wkv_p = conv_p = wkv_s = conv_s = None
    for l in range(depth):
        g = norm_g[l]
        rp = dict(mu=row(mu_shift[l]), w0=row(rwkv_w0[l]), w_up=rwkv_w_up[l].astype(BF16),
                  a0=row(rwkv_a0[l]), a_up=rwkv_a_up[l].astype(BF16),
                  g_up=rwkv_g_up[l].astype(BF16), k_k=row(rwkv_k_k[l]), k_a=row(rwkv_k_a[l]),
                  r_k=row(rwkv_r_k[l]), gn_g=row(rwkv_gn_g[l]), gn_b=row(rwkv_gn_b[l]))
        cw, cb = conv_w[l], row(conv_b[l])
        clg, clb = row(conv_ln_g[l]), row(conv_ln_b[l])

        sub = V7X_SUBLANES
        tail_p = _rms_rows(x, g[0:1], tp - sub, tp, bp, sub, name="shift_prompt")
        shift_p.append(tail_p.reshape(bp, sub, D_MODEL)[:, sub - 1])
        tail_s = _rms_rows(x, g[0:1], n_p, n_s, 1, n_s, name="shift_sample")
        shift_s.append(tail_s.reshape(bs, ts, D_MODEL)[:, ts - 1])

        (zp_s,) = _norm_matmul(state_shift[l], g[0:1], w_in_b, in_splits[1:], out_dtype=F32,
                               normalize=False, layer=l, name="prev_proj")

        zr_p, mixed_p, conv_p = _in_proj_conv(x, g[0:1], w_in_b, bp, tp, zero_conv, cw, cb, clg, clb,
                                              layer=l, depth=depth, state=conv_p)
        zc_s, zr_s = _norm_matmul(x, g[0:1], w_in_b, in_splits, out_dtype=F32, row0=n_p, n_rows=n_s,
                                  layer=l, name="in_proj_sample")
        mixed_s, conv_s = _conv_group(zc_s, bs, ts, sample_nb, state_conv[l], cw, cb, clg, clb,
                                      layer=l, depth=depth, state=conv_s, name="conv_sample")
        mixed_p, wkv_p = _rwkv_group(zr_p, 0, bp, tp, prompt_nb, 1, RWKV_GROUP_ROWS, zero_prev,
                                     zero_wkv, rp, seg, layer=l, depth=depth, mixed=mixed_p,
                                     state=wkv_p, name="rwkv_prompt")
        mixed_s, wkv_s = _rwkv_group(zr_s, 0, bs, ts, 1, RWKV_GROUP_ROWS // ts, ts, zp_s[:, None, :],
                                     state_wkv[l], rp, seg, layer=l, depth=depth, mixed=mixed_s,
                                     state=wkv_s, name="rwkv_sample")

        x, q = _res_proj([mixed_p.reshape(n_p, D_MODEL), mixed_s], w_out_b, x, g[1:2], g_b=g[2:3],
                         w_b=w_q_b, splits=[(0, D_MODEL)], out_dtype=BF16, layer=l, name="out_q_proj")
        mk2, mv2, mk5, mv5 = _mem_kv_proj(mem2d, row(mem_norm_g), w_k_b, w_v_b, layer=l, depth=depth,
                                          k5=mk5, v5=mv5)
        att_p = _attn_group(q, 0, bp, tp, min(ATTN_TILE, tp), 1, mk2.reshape(bp, N_MEM, D_MODEL),
                            mv2.reshape(bp, N_MEM, D_MODEL), layer=l, name="attn_prompt")
        att_s = _attn_group(q, n_p, bs, ts, ts, ATTN_SAMPLE_BATCH, cache_mem_k, cache_mem_v,
                            layer=l, name="attn_sample")
        x, f = _res_proj([att_p, att_s], w_o_b, x, g[3:4], g_b=g[4:5], w_b=w_up_b, splits=[(0, D_FF)],
                         relu2=True, out_dtype=BF16, layer=l, name="attn_out_ffn_up")
        if l + 1 < depth:
            (x,) = _res_proj([f], w_down_b, x, g[5:6], layer=l, name="ffn_down")
        else:
            (y_p,) = _res_proj([f], w_down_b, x, g[5:6], row0=0, n_rows=n_p, layer=l,
                               name="ffn_down_prompt")
            (y_s,) = _res_proj([f], w_down_b, x, g[5:6], row0=n_p, n_rows=n_s, layer=l,
                               name="ffn_down_sample")

    return (y_p.reshape(bp, tp, D_MODEL), y_s.reshape(bs, ts, D_MODEL), mk5, mv5, wkv_p, conv_p,
            jnp.stack(shift_p), wkv_s, conv_s, jnp.stack(shift_s))
```

```python
import functools
import math

import jax
import jax.numpy as jnp
from jax import lax
from jax.experimental import pallas as pl
from jax.experimental.pallas import tpu as pltpu

F32 = jnp.float32
BF16 = jnp.bfloat16

D_MODEL = 1024
CONV_CH = 512
RWKV_DIM = 512
RWKV_HEAD = 64
RWKV_HEADS = 8
CONV_WIDTH = 31
CONV_BUF = CONV_WIDTH - 1
DECAY_RANK = 64
ICLR_RANK = 64
GATE_RANK = 128
RWKV_COLS = 3 * RWKV_DIM + DECAY_RANK + ICLR_RANK + GATE_RANK
IN_COLS = 2 * CONV_CH + RWKV_COLS
N_MEM = 256
MEM_HEADS = 4
MEM_HEAD_DIM = D_MODEL // MEM_HEADS
D_FF = 4 * D_MODEL
RMS_EPS = 1e-6
LN_EPS = 1e-5
GN_EPS = 64e-5

V7X_LANES = 128
V7X_SUBLANES = 8
V7X_VMEM_LIMIT_BYTES = 56 * 1024 * 1024

ROW_BLOCK = 512
COL_CHUNK = 1024
RWKV_GROUP_ROWS = 64
RWKV_PROMPT_SEQS = 4
PACK_HEADS = 4
SAMPLE_STEP_ROWS = 64
INV_BASE = 8
ATTN_TILE = 512
ATTN_SAMPLE_BATCH = 4

_NT = (((1,), (1,)), ((), ()))
_TN = (((0,), (0,)), ((), ()))


def _params(semantics):
    return pltpu.CompilerParams(dimension_semantics=semantics,
                                vmem_limit_bytes=V7X_VMEM_LIMIT_BYTES)


def _dot(a, b):
    return jnp.dot(a, b, preferred_element_type=F32)


def _mm(a, b, dims=None):
    if dims is None:
        dims = (((a.ndim - 1,), (0,)), ((), ()))
    return lax.dot_general(a.astype(BF16), b.astype(BF16), dims, preferred_element_type=F32)


def _split2(x):
    hi = x.astype(BF16)
    lo = (x - hi.astype(F32)).astype(BF16)
    return hi, lo


def _split3(x):
    hi = x.astype(BF16)
    r1 = x - hi.astype(F32)
    mid = r1.astype(BF16)
    lo = (r1 - mid.astype(F32)).astype(BF16)
    return hi, mid, lo


def _rms(x, g):
    ms = jnp.mean(x * x, axis=-1, keepdims=True)
    return x * lax.rsqrt(ms + RMS_EPS) * g


def _drop_carried(body, n_in, n_carried):
    def wrapped(*refs):
        return body(*refs[:n_in], *refs[n_in + n_carried:])
    return wrapped


def _call_with_carried(body, *, grid, in_specs, inputs, out_specs, out_shape, carried,
                       scratch_shapes=(), semantics, name):
    n_in = len(inputs)
    extra, aliases = [], {}
    for out_idx, buf in sorted(carried.items()):
        if buf is None:
            continue
        aliases[n_in + len(extra)] = out_idx
        extra.append(buf)
    return pl.pallas_call(
        _drop_carried(body, n_in, len(extra)),
        grid=grid,
        in_specs=list(in_specs) + [pl.BlockSpec(memory_space=pl.ANY)] * len(extra),
        out_specs=out_specs, out_shape=out_shape, scratch_shapes=list(scratch_shapes),
        input_output_aliases=aliases, compiler_params=_params(semantics), name=name,
    )(*inputs, *extra)


def _resident(shape):
    return pl.BlockSpec(shape, lambda i: (0,) * len(shape), pipeline_mode=pl.Buffered(1))


def _weight_spec(w, layer):
    if w.ndim == 2:
        return _resident(w.shape)
    return pl.BlockSpec((None,) + w.shape[1:], lambda i: (layer, 0, 0), pipeline_mode=pl.Buffered(1))


def _project(hb, w_ref, o_refs, splits, relu2):
    for (lo, hi), o_ref in zip(splits, o_refs):
        for c0 in range(lo, hi, COL_CHUNK):
            c1 = min(hi, c0 + COL_CHUNK)
            y = _dot(hb, w_ref[:, c0:c1])
            if relu2:
                y = jnp.square(jnp.maximum(y, 0.0))
            o_ref[:, c0 - lo:c1 - lo] = y.astype(o_ref.dtype)


def _norm_matmul_body(x_ref, g_ref, w_ref, *o_refs, splits, normalize, relu2):
    x = x_ref[...]
    h = _rms(x, g_ref[...]) if normalize else x
    _project(h.astype(BF16), w_ref, o_refs, splits, relu2)


def _norm_matmul(x, g, w, splits, *, out_dtype, normalize=True, relu2=False, row0=0, n_rows=None,
                 layer=None, name):
    k = x.shape[1]
    m = x.shape[0] if n_rows is None else n_rows
    bm = math.gcd(ROW_BLOCK, m, row0)
    return pl.pallas_call(
        functools.partial(_norm_matmul_body, splits=tuple(splits), normalize=normalize,
                          relu2=relu2),
        grid=(m // bm,),
        in_specs=[pl.BlockSpec((bm, k), lambda i: (row0 // bm + i, 0)), _resident((1, k)),
                  _weight_spec(w, layer)],
        out_specs=[pl.BlockSpec((bm, hi - lo), lambda i: (i, 0)) for lo, hi in splits],
        out_shape=[jax.ShapeDtypeStruct((m, hi - lo), out_dtype) for lo, hi in splits],
        compiler_params=_params(("parallel",)), name=name,
    )(x, g, w)


def _res_proj_body(*refs, n_lhs, n_x, first_blocks, splits, relu2):
    y_refs = refs[:n_lhs]
    w_a_ref = refs[n_lhs]
    x_refs = refs[n_lhs + 1:n_lhs + 1 + n_x]
    g_a_ref = refs[n_lhs + 1 + n_x]
    rest = refs[n_lhs + 2 + n_x:]
    if splits:
        g_b_ref, w_b_ref = rest[:2]
        rest = rest[2:]
    o_ref, p_refs = rest[0], rest[1:]

    def emit(y_ref, x_ref):
        x_new = x_ref[...] + _rms(_dot(y_ref[...], w_a_ref[...]), g_a_ref[...])
        o_ref[...] = x_new
        if splits:
            _project(_rms(x_new, g_b_ref[...]).astype(BF16), w_b_ref, p_refs, splits, relu2)

    if n_lhs == 1:
        emit(y_refs[0], x_refs[0])
    else:
        i = pl.program_id(0)
        pl.when(i < first_blocks)(lambda: emit(y_refs[0], x_refs[0]))
        pl.when(i >= first_blocks)(lambda: emit(y_refs[1], x_refs[-1]))


def _res_proj(lhs, w_a, x, g_a, *, g_b=None, w_b=None, splits=(), relu2=False, out_dtype=None,
              row0=0, n_rows=None, layer=None, name):
    xs = list(x) if isinstance(x, (list, tuple)) else [x]
    m, n = sum(a.shape[0] for a in xs), xs[0].shape[1]
    k = w_a.shape[-2]
    n_rows = m if n_rows is None else n_rows
    bm = math.gcd(ROW_BLOCK, n_rows, row0, *[a.shape[0] for a in lhs])
    if len(lhs) == 1:
        assert len(xs) == 1
        lhs_specs = [pl.BlockSpec((bm, k), lambda i: (row0 // bm + i, 0))]
        first_blocks = None
    else:
        assert row0 == 0 and n_rows == m and sum(a.shape[0] for a in lhs) == m
        first_blocks = lhs[0].shape[0] // bm
        pair = lambda width: [
            pl.BlockSpec((bm, width), lambda i: (jnp.minimum(i, first_blocks - 1), 0)),
            pl.BlockSpec((bm, width), lambda i: (jnp.maximum(i - first_blocks, 0), 0))]
        lhs_specs = pair(k)
    if len(xs) == 2:
        assert [a.shape[0] for a in xs] == [a.shape[0] for a in lhs]
        x_specs = pair(n)
    else:
        x_specs = [pl.BlockSpec((bm, n), lambda i: (row0 // bm + i, 0))]
    splits = tuple(splits)
    row_out = lambda width: pl.BlockSpec((bm, width), lambda i: (i, 0))
    proj_in = [_resident((1, n)), _weight_spec(w_b, layer)] if splits else []
    proj_args = [g_b, w_b] if splits else []
    return pl.pallas_call(
        functools.partial(_res_proj_body, n_lhs=len(lhs), n_x=len(xs), first_blocks=first_blocks,
                          splits=splits, relu2=relu2),
        grid=(n_rows // bm,),
        in_specs=lhs_specs + [_weight_spec(w_a, layer)] + x_specs + [_resident((1, n))] + proj_in,
        out_specs=[row_out(n)] + [row_out(hi - lo) for lo, hi in splits],
        out_shape=[jax.ShapeDtypeStruct((n_rows, n), F32)]
        + [jax.ShapeDtypeStruct((n_rows, hi - lo), out_dtype) for lo, hi in splits],
        compiler_params=_params(("parallel",)), name=name,
    )(*lhs, w_a, *xs, g_a, *proj_args)


def _rms_rows_body(x_ref, g_ref, o_ref):
    o_ref[...] = _rms(x_ref[...], g_ref[...])


def _rms_rows(x, g, row0, stride, n_blocks, bm, *, name):
    k = x.shape[1]
    assert row0 % bm == 0 and stride % bm == 0
    return pl.pallas_call(
        _rms_rows_body, grid=(n_blocks,),
        in_specs=[pl.BlockSpec((bm, k), lambda i: (row0 // bm + i * (stride // bm), 0)),
                  _resident((1, k))],
        out_specs=pl.BlockSpec((bm, k), lambda i: (i, 0)),
        out_shape=jax.ShapeDtypeStruct((n_blocks * bm, k), F32),
        compiler_params=_params(("parallel",)), name=name,
    )(x, g)


def _mem_kv_body(x_ref, g_ref, wk_ref, wv_ref, k2_ref, v2_ref, k5_ref, v5_ref, *, nb):
    hb = _rms(x_ref[...], g_ref[...]).astype(BF16)
    for w_ref, o2_ref, o5_ref in ((wk_ref, k2_ref, k5_ref), (wv_ref, v2_ref, v5_ref)):
        y = _dot(hb, w_ref[...])
        o2_ref[...] = y.astype(o2_ref.dtype)
        for i in range(nb):
            for h in range(MEM_HEADS):
                o5_ref[i, :, h, :] = y[i * N_MEM:(i + 1) * N_MEM,
                                       h * MEM_HEAD_DIM:(h + 1) * MEM_HEAD_DIM]


def _mem_kv_proj(mem2d, g, wk, wv, *, layer, depth, k5, v5):
    m, k = mem2d.shape
    nb = ROW_BLOCK // N_MEM
    batch = m // N_MEM
    assert batch % nb == 0
    row_spec = pl.BlockSpec((nb * N_MEM, D_MODEL), lambda i: (i, 0))
    w_spec = _weight_spec(wk, layer)
    out5 = pl.BlockSpec((None, nb, N_MEM, MEM_HEADS, MEM_HEAD_DIM), lambda i: (layer, i, 0, 0, 0))
    shape5 = jax.ShapeDtypeStruct((depth, batch, N_MEM, MEM_HEADS, MEM_HEAD_DIM), F32)
    return _call_with_carried(
        functools.partial(_mem_kv_body, nb=nb),
        grid=(batch // nb,),
        in_specs=[row_spec, pl.BlockSpec((1, k), lambda i: (0, 0)), w_spec, w_spec],
        inputs=[mem2d, g, wk, wv],
        out_specs=[row_spec, row_spec, out5, out5],
        out_shape=[jax.ShapeDtypeStruct((m, D_MODEL), BF16), jax.ShapeDtypeStruct((m, D_MODEL), BF16),
                   shape5, shape5],
        carried={2: k5, 3: v5}, semantics=("parallel",), name="mem_kv_proj")


CONV_CUR_ROW = CONV_BUF + (-CONV_BUF) % V7X_SUBLANES
CONV_HALO_ROW = CONV_CUR_ROW - CONV_BUF


def _conv_tile(z, ext_ref, sh_ref, w_ref, b_ref, lg_ref, lb_ref):
    tt = z.shape[0]
    h0, cur = CONV_HALO_ROW, CONV_CUR_ROW
    ext_ref[cur:cur + tt, :] = z[:, :CONV_CH] * jax.nn.sigmoid(z[:, CONV_CH:])
    acc = None
    for r in range(V7X_SUBLANES):
        taps = range(r, CONV_WIDTH, V7X_SUBLANES)
        span = tt + taps[-1] - r
        sh_ref[r, 0:span, :] = ext_ref[h0 + r:h0 + r + span, :]
        for k in taps:
            term = sh_ref[r, k - r:k - r + tt, :] * w_ref[k:k + 1, :]
            acc = term if acc is None else acc + term
    y = acc + b_ref[...]
    mu = jnp.mean(y, axis=-1, keepdims=True)
    var = jnp.mean(jnp.square(y - mu), axis=-1, keepdims=True)
    yn = (y - mu) * lax.rsqrt(var + LN_EPS) * lg_ref[...] + lb_ref[...]
    return yn * jax.nn.sigmoid(yn), ext_ref[tt + h0:tt + cur, :]


def _conv_scratch(tt):
    return [pltpu.VMEM((CONV_CUR_ROW + tt, CONV_CH), F32),
            pltpu.VMEM((V7X_SUBLANES, tt + CONV_BUF, CONV_CH), F32)]


def _conv_body(z_ref, buf_ref, w_ref, b_ref, lg_ref, lb_ref, y_ref, nc_ref, ext_ref, sh_ref,
               *, nb, tt):
    for i in range(nb):
        rows = slice(i * tt, (i + 1) * tt)
        ext_ref[CONV_HALO_ROW:CONV_CUR_ROW, :] = buf_ref[i]
        out, halo = _conv_tile(z_ref[rows, :], ext_ref, sh_ref, w_ref, b_ref, lg_ref, lb_ref)
        y_ref[rows, :] = out.astype(y_ref.dtype)
        nc_ref[i] = halo


def _conv_group(zc, batch, seq, nb, conv_buf, w, b, lg, lb, *, layer, depth, state, name):
    rows = nb * seq
    assert batch % nb == 0 and zc.shape[0] == batch * seq
    const = lambda bi: (0, 0)
    return _call_with_carried(
        functools.partial(_conv_body, nb=nb, tt=seq),
        grid=(batch // nb,),
        in_specs=[pl.BlockSpec((rows, 2 * CONV_CH), lambda bi: (bi, 0)),
                  pl.BlockSpec((nb, CONV_BUF, CONV_CH), lambda bi: (bi, 0, 0)),
                  pl.BlockSpec((CONV_WIDTH, CONV_CH), const),
                  pl.BlockSpec((1, CONV_CH), const), pl.BlockSpec((1, CONV_CH), const),
                  pl.BlockSpec((1, CONV_CH), const)],
        inputs=[zc, conv_buf, w, b, lg, lb],
        out_specs=[pl.BlockSpec((rows, CONV_CH), lambda bi: (bi, 0)),
                   pl.BlockSpec((None, nb, CONV_BUF, CONV_CH), lambda bi: (layer, bi, 0, 0))],
        out_shape=[jax.ShapeDtypeStruct((batch * seq, CONV_CH + RWKV_DIM), BF16),
                   jax.ShapeDtypeStruct((depth, batch, CONV_BUF, CONV_CH), F32)],
        carried={1: state}, scratch_shapes=_conv_scratch(seq),
        semantics=("parallel",), name=name)


def _in_proj_conv_body(x_ref, g_ref, w_ref, buf_ref, cw_ref, cb_ref, lg_ref, lb_ref,
                       zr_ref, y_ref, nc_ref, ext_ref, sh_ref, *, tiles_per_seq):
    @pl.when(pl.program_id(0) % tiles_per_seq == 0)
    def _():
        ext_ref[CONV_HALO_ROW:CONV_CUR_ROW, :] = buf_ref[0]

    hb = _rms(x_ref[...], g_ref[...]).astype(BF16)
    z = _dot(hb, w_ref[:, 0:2 * CONV_CH])
    out, halo = _conv_tile(z, ext_ref, sh_ref, cw_ref, cb_ref, lg_ref, lb_ref)
    y_ref[...] = out.astype(y_ref.dtype)
    ext_ref[CONV_HALO_ROW:CONV_CUR_ROW, :] = halo
    nc_ref[0] = halo
    _project(hb, w_ref, [zr_ref], [(2 * CONV_CH, IN_COLS)], False)


def _in_proj_conv(x, g, w, batch, seq, conv_buf, cw, cb, lg, lb, *, layer, depth, state):
    k = x.shape[1]
    tt = math.gcd(ROW_BLOCK, seq)
    tiles = seq // tt
    const = lambda i: (0, 0)
    return _call_with_carried(
        functools.partial(_in_proj_conv_body, tiles_per_seq=tiles),
        grid=(batch * tiles,),
        in_specs=[pl.BlockSpec((tt, k), lambda i: (i, 0)), _resident((1, k)), _weight_spec(w, layer),
                  pl.BlockSpec((1, CONV_BUF, CONV_CH), lambda i: (i // tiles, 0, 0)),
                  pl.BlockSpec((CONV_WIDTH, CONV_CH), const),
                  pl.BlockSpec((1, CONV_CH), const), pl.BlockSpec((1, CONV_CH), const),
                  pl.BlockSpec((1, CONV_CH), const)],
        inputs=[x, g, w, conv_buf, cw, cb, lg, lb],
        out_specs=[pl.BlockSpec((tt, RWKV_COLS), lambda i: (i, 0)),
                   pl.BlockSpec((None, tt, CONV_CH), lambda i: (i // tiles, i % tiles, 0)),
                   pl.BlockSpec((None, 1, CONV_BUF, CONV_CH), lambda i: (layer, i // tiles, 0, 0))],
        out_shape=[jax.ShapeDtypeStruct((batch * seq, RWKV_COLS), F32),
                   jax.ShapeDtypeStruct((batch, seq, CONV_CH + RWKV_DIM), BF16),
                   jax.ShapeDtypeStruct((depth, batch, CONV_BUF, CONV_CH), F32)],
        carried={2: state}, scratch_shapes=_conv_scratch(tt),
        semantics=("arbitrary",), name="in_proj_conv")


def _head_sums(xs, seg):
    rows = xs[0].shape[0]
    groups = RWKV_DIM // V7X_LANES
    stacked = jnp.concatenate(
        [x[:, g * V7X_LANES:(g + 1) * V7X_LANES] for x in xs for g in range(groups)], axis=0)
    hi, lo = _split2(stacked)
    y = _dot(hi, seg) + _dot(lo, seg)
    outs = []
    for i in range(len(xs)):
        parts = [y[(i * groups + g) * rows:(i * groups + g + 1) * rows] for g in range(groups)]
        outs.append(jnp.concatenate(parts, axis=1))
    return outs


def _block_diag(blocks):
    n = len(blocks)
    w = blocks[0].shape[1]
    rows = []
    for i, blk in enumerate(blocks):
        pieces = ([jnp.zeros((blk.shape[0], i * w), blk.dtype)] if i else []) + [blk]
        if i < n - 1:
            pieces.append(jnp.zeros((blk.shape[0], (n - 1 - i) * w), blk.dtype))
        rows.append(jnp.concatenate(pieces, axis=1))
    return jnp.concatenate(rows, axis=0)


def _rwkv_body(*refs, n_grp, spg, tlen, n_zr):
    zr_refs = refs[:n_zr]
    (zp0_ref, s0_ref, mu_ref, w0_ref, wup_ref, a0_ref, aup_ref, gup_ref, kk_ref, ka_ref, rk_ref,
     gng_ref, gnb_ref, seg_ref, o_ref, s_out_ref, s_scr, carry_scr) = refs[n_zr:]
    nb = n_grp * spg
    rows = nb * tlen
    gr = spg * tlen
    shift = int(math.log2(tlen))
    n_pack = RWKV_HEADS // PACK_HEADS
    pw = PACK_HEADS * RWKV_HEAD
    assert 1 << shift == tlen and gr == RWKV_HEAD == RWKV_GROUP_ROWS

    @pl.when(pl.program_id(1) == 0)
    def _():
        carry_scr[...] = zp0_ref[...]
        for q in range(nb):
            for p in range(n_pack):
                s_scr[q, p] = _block_diag([s0_ref[q, p * PACK_HEADS + h] for h in range(PACK_HEADS)])

    if n_zr == 1:
        zr = zr_refs[0][...]
    else:
        zr = jnp.concatenate([r[...] for r in zr_refs], axis=0)
    seqs = [slice(i * tlen, (i + 1) * tlen) for i in range(nb)]
    t_idx = lax.broadcasted_iota(jnp.int32, (rows, 1), 0) & (tlen - 1)
    if nb == 1:
        carry_rows = carry_scr[0]
    else:
        carry_rows = jnp.concatenate(
            [jnp.broadcast_to(carry_scr[i], (tlen, RWKV_COLS)) for i in range(nb)], axis=0)
    prev = jnp.where(t_idx == 0, carry_rows, pltpu.roll(zr, 1, axis=0))
    for i, rs in enumerate(seqs):
        carry_scr[i] = zr[rs.stop - 1:rs.stop, :]
    zs = zr + (prev - zr) * mu_ref[...]

    d = RWKV_DIM
    r = zs[:, 0:d]
    k = zs[:, d:2 * d]
    v = zs[:, 2 * d:3 * d]
    dw = zs[:, 3 * d:3 * d + DECAY_RANK]
    da = zs[:, 3 * d + DECAY_RANK:3 * d + DECAY_RANK + ICLR_RANK]
    dg = zs[:, 3 * d + DECAY_RANK + ICLR_RANK:]

    lw = w0_ref[...] + _mm(jnp.tanh(dw), wup_ref[...])
    logw = -math.exp(-0.5) * jax.nn.sigmoid(lw)
    a = jax.nn.sigmoid(a0_ref[...] + _mm(da, aup_ref[...]))
    gate = _mm(jax.nn.sigmoid(dg), gup_ref[...])

    seg = seg_ref[...]
    kk = k * kk_ref[...]
    kmod = k * (1.0 + (a - 1.0) * ka_ref[...])
    kk_sq, bonus_dot = _head_sums([kk * kk, r * kmod * rk_ref[...]], seg)
    kk = kk / jnp.maximum(jnp.sqrt(kk_sq), 1e-12)
    bonus = bonus_dot * v

    ri = lax.broadcasted_iota(jnp.int32, (rows, rows), 0)
    ci = lax.broadcasted_iota(jnp.int32, (rows, rows), 1)
    tri = (((ri >> shift) == (ci >> shift)) & (ri >= ci)).astype(BF16)
    w_hi, w_mid, w_lo = _split3(logw)
    cl = _dot(tri, w_hi) + _dot(tri, w_mid) + _dot(tri, w_lo)
    g_incl = jnp.exp(cl)
    g_inv = jnp.exp(-cl)
    a_t = -kk * jnp.exp(cl - logw)
    b_t = kk * a * g_inv
    k_t = kmod * g_inv
    r_t = r * g_incl

    pr = lax.broadcasted_iota(jnp.int32, (gr, pw), 0)
    pc = lax.broadcasted_iota(jnp.int32, (gr, pw), 1) & (RWKV_HEAD - 1)
    p_same = (pr >> shift) == (pc >> shift)
    p_incl = p_same & (pr >= pc)
    p_strict = p_same & (pr > pc)
    p_eye = (pr == pc).astype(F32)
    base_shift = min(shift, int(math.log2(INV_BASE)))
    base_blk = (pr >> base_shift) == (pc >> base_shift)
    level_blks = [((pr >> (lv + 1)) == (pc >> (lv + 1))) & ((pr >> lv) != (pc >> lv))
                  for lv in range(base_shift, shift)]
    bd_mask = ((lax.broadcasted_iota(jnp.int32, (pw, pw), 0) >> 6)
               == (lax.broadcasted_iota(jnp.int32, (pw, pw), 1) >> 6))
    bd_zero = jnp.zeros((pw, pw), BF16)

    def bd(x):
        return jnp.where(bd_mask, jnp.concatenate([x.astype(BF16)] * PACK_HEADS, axis=0), bd_zero)

    def pk(x, y_bd, dims=None):
        return _mm(x, y_bd, dims)

    chains = [(g, p) for g in range(n_grp) for p in range(n_pack)]
    nch = range(len(chains))
    cut = lambda m, g, p: m[g * gr:(g + 1) * gr, p * pw:(p + 1) * pw]
    a_p = [cut(a_t, g, p) for g, p in chains]
    b_p = [cut(b_t, g, p) for g, p in chains]
    k_p = [cut(k_t, g, p) for g, p in chains]
    r_p = [cut(r_t, g, p) for g, p in chains]
    v_p = [cut(v, g, p) for g, p in chains]
    sub = [slice(j * tlen, (j + 1) * tlen) for j in range(spg)]
    ar = [jnp.concatenate([a_p[c], r_p[c]], axis=0) for c in nch]
    b_bd = [bd(b_p[c]) for c in nch]
    k_bd = [bd(k_p[c]) for c in nch]
    v_bd = [bd(v_p[c]) for c in nch]
    gb = [pk(ar[c], b_bd[c], _NT) for c in nch]
    gk = [pk(ar[c], k_bd[c], _NT) for c in nch]
    a_ab = [jnp.where(p_strict, gb[c][:gr], 0.0) for c in nch]
    a_rb = [jnp.where(p_incl, gb[c][gr:], 0.0) for c in nch]
    a_ak = [jnp.where(p_strict, gk[c][:gr], 0.0) for c in nch]
    a_rk = [jnp.where(p_incl, gk[c][gr:], 0.0) for c in nch]
    s_old = [[s_scr[g * spg + j, p] for j in range(spg)] for g, p in chains]
    s_terms = [[pk(jnp.concatenate([a_p[c][sr], r_p[c][sr]], axis=0), s_old[c][j], _NT)
                for j, sr in enumerate(sub)] for c in nch]
    a_s0 = [jnp.concatenate([x[:tlen] for x in s_terms[c]], axis=0) for c in nch]
    r_s0 = [jnp.concatenate([x[tlen:] for x in s_terms[c]], axis=0) for c in nch]
    av = [pk(jnp.concatenate([a_ak[c], a_rk[c]], axis=0), v_bd[c]) for c in nch]
    rhs = [a_s0[c] + av[c][:gr] for c in nch]
    npow = [jnp.where(base_blk, a_ab[c], 0.0) for c in nch]
    inv = [p_eye + npow[c] for c in nch]
    for _ in range(max(0, base_shift - 1)):
        npow = [pk(npow[c], bd(npow[c])) for c in nch]
        inv = [inv[c] + pk(inv[c], bd(npow[c])) for c in nch]
    for off_blk in level_blks:
        cross = [pk(jnp.where(off_blk, a_ab[c], 0.0), bd(inv[c])) for c in nch]
        inv = [inv[c] + pk(inv[c], bd(cross[c])) for c in nch]
    u = [pk(inv[c], bd(rhs[c])) for c in nch]
    outs = [r_s0[c] + pk(a_rb[c], bd(u[c])) + av[c][gr:] for c in nch]
    for c, (g, p) in enumerate(chains):
        for j, sr in enumerate(sub):
            uv = jnp.concatenate([u[c][sr], v_p[c][sr]], axis=0)
            bk = jnp.concatenate([b_p[c][sr], k_p[c][sr]], axis=0)
            last_row = g * gr + sr.stop - 1
            g_last = g_incl[last_row:last_row + 1, p * pw:(p + 1) * pw]
            upd = jnp.where(bd_mask, pk(uv, bk, _TN), 0.0)
            s_scr[g * spg + j, p] = (s_old[c][j] + upd) * g_last
    o = jnp.concatenate(
        [jnp.concatenate([outs[g * n_pack + p] for p in range(n_pack)], axis=1)
         for g in range(n_grp)], axis=0)

    inv_n = 1.0 / RWKV_HEAD
    (o_sum,) = _head_sums([o], seg)
    cen = o - o_sum * inv_n
    (sq_sum,) = _head_sums([cen * cen], seg)
    y = cen * lax.rsqrt(sq_sum * inv_n + GN_EPS) * gng_ref[...] + gnb_ref[...] + bonus
    y = (y * gate).astype(o_ref.dtype)
    if len(o_ref.shape) == 3:
        for i, rs in enumerate(seqs):
            o_ref[i] = y[rs]
    else:
        o_ref[...] = y

    @pl.when(pl.program_id(1) == pl.num_programs(1) - 1)
    def _():
        for q in range(nb):
            for h in range(RWKV_HEADS):
                blk = slice((h % PACK_HEADS) * RWKV_HEAD, (h % PACK_HEADS + 1) * RWKV_HEAD)
                s_out_ref[q, h] = s_scr[q, h // PACK_HEADS, blk, blk]


def _rwkv_group(zr, row0, batch, seq, n_grp, spg, tlen, zp0, s0, p, seg, *, layer, depth, mixed,
                state, name):
    nb = n_grp * spg
    nc = seq // tlen
    assert seq % tlen == 0 and batch % nb == 0 and row0 % tlen == 0 and (spg == 1 or nc == 1)
    const = lambda bi, j: (0, 0)
    vec = lambda n: pl.BlockSpec((1, n), const)
    mat = lambda a: pl.BlockSpec(a.shape, const)
    st = (nb, RWKV_HEADS, RWKV_HEAD, RWKV_HEAD)
    base = row0 // tlen
    if nc > 1:
        zr_specs = [pl.BlockSpec((tlen, RWKV_COLS),
                                 lambda bi, j, i=i: (base + (bi * nb + i) * nc + j, 0))
                    for i in range(nb)]
    else:
        assert row0 % (nb * tlen) == 0
        zr_specs = [pl.BlockSpec((nb * tlen, RWKV_COLS),
                                 lambda bi, j: (row0 // (nb * tlen) + bi, 0))]
    if mixed.ndim == 3:
        o_spec = pl.BlockSpec((nb, tlen, RWKV_DIM), lambda bi, j: (bi, j, 1))
    else:
        assert nc == 1
        o_spec = pl.BlockSpec((nb * tlen, RWKV_DIM), lambda bi, j: (bi, 1))
    n_zr = len(zr_specs)
    return _call_with_carried(
        functools.partial(_rwkv_body, n_grp=n_grp, spg=spg, tlen=tlen, n_zr=n_zr),
        grid=(batch // nb, nc),
        in_specs=zr_specs + [
            pl.BlockSpec((nb, 1, RWKV_COLS), lambda bi, j: (bi, 0, 0)),
            pl.BlockSpec(st, lambda bi, j: (bi, 0, 0, 0)),
            vec(RWKV_COLS), vec(RWKV_DIM), mat(p["w_up"]), vec(RWKV_DIM), mat(p["a_up"]),
            mat(p["g_up"]), vec(RWKV_DIM), vec(RWKV_DIM), vec(RWKV_DIM), vec(RWKV_DIM),
            vec(RWKV_DIM), mat(seg)],
        inputs=[zr] * n_zr + [zp0, s0, p["mu"], p["w0"], p["w_up"], p["a0"], p["a_up"], p["g_up"],
                              p["k_k"], p["k_a"], p["r_k"], p["gn_g"], p["gn_b"], seg],
        out_specs=[o_spec, pl.BlockSpec((None,) + st, lambda bi, j: (layer, bi, 0, 0, 0))],
        out_shape=[jax.ShapeDtypeStruct(mixed.shape, BF16),
                   jax.ShapeDtypeStruct((depth, batch) + st[1:], F32)],
        carried={0: mixed, 1: state},
        scratch_shapes=[pltpu.VMEM((nb, RWKV_HEADS // PACK_HEADS, PACK_HEADS * RWKV_HEAD,
                                    PACK_HEADS * RWKV_HEAD), F32),
                        pltpu.VMEM((nb, 1, RWKV_COLS), F32)],
        semantics=("parallel", "arbitrary"), name=name)


def _attn_body(q_ref, k_ref, v_ref, o_ref, *dense_refs, bb, tq):
    scale = MEM_HEAD_DIM ** -0.5
    pairs = [(i, h) for i in range(bb) for h in range(MEM_HEADS)]
    rows = lambda i: slice(i * tq, (i + 1) * tq)
    cols = lambda h: slice(h * MEM_HEAD_DIM, (h + 1) * MEM_HEAD_DIM)

    if dense_refs:
        for src, dst in zip((k_ref, v_ref), dense_refs):
            for i, h in pairs:
                dst[i, :, cols(h)] = src[i, :, h, :]
        k_ref, v_ref = dense_refs

    def head(ref, i, h):
        return ref[i, :, cols(h)].astype(BF16)

    s = [lax.dot_general(q_ref[rows(i), cols(h)], head(k_ref, i, h), _NT,
                         preferred_element_type=F32) * scale for i, h in pairs]
    e = [jnp.exp(x - jnp.max(x, axis=-1, keepdims=True)) for x in s]
    p = [x / jnp.sum(x, axis=-1, keepdims=True) for x in e]
    o = [_dot(p[n].astype(BF16), head(v_ref, i, h)) for n, (i, h) in enumerate(pairs)]
    for n, (i, h) in enumerate(pairs):
        o_ref[rows(i), cols(h)] = o[n].astype(o_ref.dtype)


def _attn_group(q, row0, batch, seq, tq, bb, mk, mv, *, layer, name):
    nt = seq // tq
    rows = bb * tq
    base = row0 // rows
    assert seq % tq == 0 and batch % bb == 0 and row0 % rows == 0 and (bb == 1 or nt == 1)
    if mk.ndim == 5:
        kv_spec = pl.BlockSpec((None, bb, N_MEM, MEM_HEADS, MEM_HEAD_DIM),
                               lambda bi, j: (layer, bi, 0, 0, 0))
        scratch = [pltpu.VMEM((bb, N_MEM, D_MODEL), F32)] * 2
    else:
        kv_spec = pl.BlockSpec((bb, N_MEM, D_MODEL), lambda bi, j: (bi, 0, 0))
        scratch = []
    return pl.pallas_call(
        functools.partial(_attn_body, bb=bb, tq=tq),
        grid=(batch // bb, nt),
        in_specs=[pl.BlockSpec((rows, D_MODEL), lambda bi, j: (base + bi * nt + j, 0)),
                  kv_spec, kv_spec],
        out_specs=pl.BlockSpec((rows, D_MODEL), lambda bi, j: (bi * nt + j, 0)),
        out_shape=jax.ShapeDtypeStruct((batch * seq, D_MODEL), BF16),
        scratch_shapes=scratch,
        compiler_params=_params(("parallel", "arbitrary")), name=name,
    )(q, mk, mv)


def kernel(x_prompt, x_sample, cache_mem_k, cache_mem_v, state_wkv, state_conv, state_shift,
           mem_prompt, w_in, mu_shift, conv_w, conv_b, conv_ln_g, conv_ln_b, rwkv_w0, rwkv_w_up,
           rwkv_a0, rwkv_a_up, rwkv_g_up, rwkv_k_k, rwkv_k_a, rwkv_r_k, rwkv_gn_g, rwkv_gn_b,
           w_out, mem_norm_g, w_q, w_k, w_v, w_o, w_ffn1, w_ffn2, norm_g):
    bp, tp, _ = x_prompt.shape
    bs, ts, _ = x_sample.shape
    depth = w_in.shape[0]
    n_p, n_s = bp * tp, bs * ts
    row = lambda a: a.reshape(1, -1)
    sample_nb = SAMPLE_STEP_ROWS // ts
    prompt_nb = min(RWKV_PROMPT_SEQS, bp)

    x = [x_prompt.reshape(n_p, D_MODEL), x_sample.reshape(n_s, D_MODEL)]
    mem2d = mem_prompt.reshape(bp * N_MEM, D_MODEL)
    lane_head = jnp.arange(V7X_LANES) // RWKV_HEAD
    seg = (lane_head[:, None] == lane_head[None, :]).astype(BF16)
    zero_conv = jnp.zeros((bp, CONV_BUF, CONV_CH), F32)
    zero_prev = jnp.zeros((bp, 1, RWKV_COLS), F32)
    zero_wkv = jnp.zeros((bp, RWKV_HEADS, RWKV_HEAD, RWKV_HEAD), F32)

    in_splits = [(0, 2 * CONV_CH), (2 * CONV_CH, IN_COLS)]
    w_in_b, w_out_b, w_q_b, w_k_b, w_v_b, w_o_b, w_up_b, w_down_b = (
        w.astype(BF16) for w in (w_in, w_out, w_q, w_k, w_v, w_o, w_ffn1, w_ffn2))

    shift_p, shift_s = [], []
    mk5 = mv5 = wkv_p = conv_p = wkv_s = conv_s = None
    for l in range(depth):
        g = norm_g[l]
        rp = dict(mu=row(mu_shift[l]), w0=row(rwkv_w0[l]), w_up=rwkv_w_up[l].astype(BF16),
                  a0=row(rwkv_a0[l]), a_up=rwkv_a_up[l].astype(BF16),
                  g_up=rwkv_g_up[l].astype(BF16), k_k=row(rwkv_k_k[l]), k_a=row(rwkv_k_a[l]),
                  r_k=row(rwkv_r_k[l]), gn_g=row(rwkv_gn_g[l]), gn_b=row(rwkv_gn_b[l]))
        cw, cb = conv_w[l], row(conv_b[l])
        clg, clb = row(conv_ln_g[l]), row(conv_ln_b[l])

        sub = V7X_SUBLANES
        x_p, x_s, s_row0 = (x[0], x[1], 0) if isinstance(x, list) else (x, x, n_p)
        tail_p = _rms_rows(x_p, g[0:1], tp - sub, tp, bp, sub, name="shift_prompt")
        shift_p.append(tail_p.reshape(bp, sub, D_MODEL)[:, sub - 1])
        tail_s = _rms_rows(x_s, g[0:1], s_row0, n_s, 1, n_s, name="shift_sample")
        shift_s.append(tail_s.reshape(bs, ts, D_MODEL)[:, ts - 1])

        (zp_s,) = _norm_matmul(state_shift[l], g[0:1], w_in_b, in_splits[1:], out_dtype=F32,
                               normalize=False, layer=l, name="prev_proj")

        zr_p, mixed_p, conv_p = _in_proj_conv(x_p, g[0:1], w_in_b, bp, tp, zero_conv, cw, cb, clg, clb,
                                              layer=l, depth=depth, state=conv_p)
        zc_s, zr_s = _norm_matmul(x_s, g[0:1], w_in_b, in_splits, out_dtype=F32, row0=s_row0, n_rows=n_s,
                                  layer=l, name="in_proj_sample")
        mixed_s, conv_s = _conv_group(zc_s, bs, ts, sample_nb, state_conv[l], cw, cb, clg, clb,
                                      layer=l, depth=depth, state=conv_s, name="conv_sample")
        mixed_p, wkv_p = _rwkv_group(zr_p, 0, bp, tp, prompt_nb, 1, RWKV_GROUP_ROWS, zero_prev,
                                     zero_wkv, rp, seg, layer=l, depth=depth, mixed=mixed_p,
                                     state=wkv_p, name="rwkv_prompt")
        mixed_s, wkv_s = _rwkv_group(zr_s, 0, bs, ts, 1, RWKV_GROUP_ROWS // ts, ts, zp_s[:, None, :],
                                     state_wkv[l], rp, seg, layer=l, depth=depth, mixed=mixed_s,
                                     state=wkv_s, name="rwkv_sample")

        x, q = _res_proj([mixed_p.reshape(n_p, D_MODEL), mixed_s], w_out_b, x, g[1:2], g_b=g[2:3],
                         w_b=w_q_b, splits=[(0, D_MODEL)], out_dtype=BF16, layer=l, name="out_q_proj")
        mk2, mv2, mk5, mv5 = _mem_kv_proj(mem2d, row(mem_norm_g), w_k_b, w_v_b, layer=l, depth=depth,
                                          k5=mk5, v5=mv5)
        att_p = _attn_group(q, 0, bp, tp, min(ATTN_TILE, tp), 1, mk2.reshape(bp, N_MEM, D_MODEL),
                            mv2.reshape(bp, N_MEM, D_MODEL), layer=l, name="attn_prompt")
        att_s = _attn_group(q, n_p, bs, ts, ts, ATTN_SAMPLE_BATCH, cache_mem_k, cache_mem_v,
                            layer=l, name="attn_sample")
        x, f = _res_proj([att_p, att_s], w_o_b, x, g[3:4], g_b=g[4:5], w_b=w_up_b, splits=[(0, D_FF)],
                         relu2=True, out_dtype=BF16, layer=l, name="attn_out_ffn_up")
        if l + 1 < depth:
            (x,) = _res_proj([f], w_down_b, x, g[5:6], layer=l, name="ffn_down")
        else:
            (y_p,) = _res_proj([f], w_down_b, x, g[5:6], row0=0, n_rows=n_p, layer=l,
                               name="ffn_down_prompt")
            (y_s,) = _res_proj([f], w_down_b, x, g[5:6], row0=n_p, n_rows=n_s, layer=l,
                               name="ffn_down_sample")

    return (y_p.reshape(bp, tp, D_MODEL), y_s.reshape(bs, ts, D_MODEL), mk5, mv5, wkv_p, conv_p,
            jnp.stack(shift_p), wkv_s, conv_s, jnp.stack(shift_s))
```
